```python
import math
import jax, jax.numpy as jnp
from jax import lax
import numpy as np

D_MODEL = 1024
BATCH = 16
SEQ = 2048
DEPTH = 2

HEAD_DIM = 64
N_HEADS_DIFF = 4
N_HEADS_FOX = 4
N_HEADS_MOBA = 4
N_BRANCH = 3
DIFF_QK_WIDTH = N_HEADS_DIFF * 2 * HEAD_DIM
DIFF_V_WIDTH = N_HEADS_DIFF * 2 * HEAD_DIM
FOX_WIDTH = N_HEADS_FOX * HEAD_DIM
MOBA_WIDTH = N_HEADS_MOBA * HEAD_DIM
D_IN = 2 * DIFF_QK_WIDTH + DIFF_V_WIDTH + 3 * FOX_WIDTH + 3 * MOBA_WIDTH + N_BRANCH * D_MODEL + N_HEADS_FOX
D_FF = ((8 * D_MODEL + 3 * 256 - 1) // (3 * 256)) * 256
ROPE_THETA = 10000.0
Q_BLOCK = 128
MOBA_BLOCK = 256
MOBA_TOPK = 3
MOBA_Q_CHUNK = 32
RMS_EPS = 1e-6
NEG_INF = -1e30

kernel_name = 'hybrid_diff_fox_moba_block'


def rmsnorm(x, g):
    xf = x.astype(jnp.float32)
    y = xf * lax.rsqrt(jnp.mean(xf * xf, axis=-1, keepdims=True) + RMS_EPS)
    return (y * g.astype(jnp.float32)).astype(x.dtype)


def rope_tables(positions):
    inv_freq = 1.0 / (ROPE_THETA ** (jnp.arange(0, HEAD_DIM, 2, dtype=jnp.float32) / HEAD_DIM))
    ang = positions.astype(jnp.float32)[..., None] * inv_freq
    return jnp.cos(ang)[:, :, None, :], jnp.sin(ang)[:, :, None, :]


def apply_rope(x, cos, sin):
    xf = x.astype(jnp.float32)
    x1, x2 = jnp.split(xf, 2, axis=-1)
    out = jnp.concatenate([x1 * cos - x2 * sin, x2 * cos + x1 * sin], axis=-1)
    return out.astype(x.dtype)


def causal_mask(q_start, n_q, n_k):
    qpos = q_start + jnp.arange(n_q)
    return jnp.arange(n_k)[None, :] <= qpos[:, None]


def diff_attention(q, k, v, lam, lam_init, g_subln):
    B, H, _, S, dh = q.shape
    kf = k.astype(jnp.float32)
    scale = dh ** -0.5

    def one_block(i):
        start = i * Q_BLOCK
        qb = lax.dynamic_slice_in_dim(q, start, Q_BLOCK, axis=3).astype(jnp.float32)
        logits = jnp.einsum('bhmqd,bhmkd->bhmqk', qb, kf) * scale
        logits = jnp.where(causal_mask(start, Q_BLOCK, S), logits, NEG_INF)
        p = jax.nn.softmax(logits, axis=-1)
        w = p[:, :, 0] - lam * p[:, :, 1]
        return jnp.einsum('bhqk,bhkd->bhqd', w.astype(v.dtype), v)

    o = lax.map(one_block, jnp.arange(S // Q_BLOCK))
    o = o.transpose(1, 2, 0, 3, 4).reshape(B, H, S, v.shape[-1])
    return rmsnorm(o, g_subln) * (1.0 - lam_init)


def forgetting_attention(q, k, v, log_f):
    B, H, S, dh = q.shape
    kf = k.astype(jnp.float32)
    cum = jnp.cumsum(log_f, axis=-1)
    scale = dh ** -0.5

    def one_block(i):
        start = i * Q_BLOCK
        qb = lax.dynamic_slice_in_dim(q, start, Q_BLOCK, axis=2).astype(jnp.float32)
        cq = lax.dynamic_slice_in_dim(cum, start, Q_BLOCK, axis=2)
        logits = jnp.einsum('bhqd,bhkd->bhqk', qb, kf) * scale
        logits = logits + cq[..., None] - cum[:, :, None, :]
        logits = jnp.where(causal_mask(start, Q_BLOCK, S), logits, NEG_INF)
        p = jax.nn.softmax(logits, axis=-1)
        return jnp.einsum('bhqk,bhkd->bhqd', p.astype(v.dtype), v)

    o = lax.map(one_block, jnp.arange(S // Q_BLOCK))
    return o.transpose(1, 2, 0, 3, 4).reshape(B, H, S, dh)


def moba_attention(q, k, v):
    B, H, S, dh = q.shape
    nb = -(-S // MOBA_BLOCK)
    pad = nb * MOBA_BLOCK - S
    kb = jnp.pad(k, ((0, 0), (0, 0), (0, pad), (0, 0))).reshape(B, H, nb, MOBA_BLOCK, dh)
    vb = jnp.pad(v, ((0, 0), (0, 0), (0, pad), (0, 0))).reshape(B, H, nb, MOBA_BLOCK, dh)
    kmean = jnp.mean(kb.astype(jnp.float32), axis=3)
    ksel = min(MOBA_TOPK, nb - 1)
    scale = dh ** -0.5
    b_idx = jnp.arange(B)[:, None, None, None]
    h_idx = jnp.arange(H)[None, :, None, None]

    def one_chunk(i):
        start = i * MOBA_Q_CHUNK
        qf = lax.dynamic_slice_in_dim(q, start, MOBA_Q_CHUNK, axis=2).astype(jnp.float32)
        qpos = start + jnp.arange(MOBA_Q_CHUNK)
        own = start // MOBA_BLOCK
        k_own = lax.dynamic_index_in_dim(kb, own, axis=2, keepdims=False)
        v_own = lax.dynamic_index_in_dim(vb, own, axis=2, keepdims=False)
        kpos_own = own * MOBA_BLOCK + jnp.arange(MOBA_BLOCK)
        lo = jnp.einsum('bhqd,bhkd->bhqk', qf, k_own.astype(jnp.float32)) * scale
        lo = jnp.where(kpos_own[None, :] <= qpos[:, None], lo, NEG_INF)
        if ksel == 0:
            p = jax.nn.softmax(lo, axis=-1)
            return jnp.einsum('bhqk,bhkd->bhqd', p.astype(v.dtype), v_own)
        gate = jnp.einsum('bhqd,bhnd->bhqn', qf, kmean)
        gate = jnp.where(jnp.arange(nb) < own, gate, NEG_INF)
        _, top_i = lax.top_k(gate, ksel)
        valid = top_i < own
        kg = kb[b_idx, h_idx, top_i]
        vg = vb[b_idx, h_idx, top_i]
        ls = jnp.einsum('bhqd,bhqnkd->bhqnk', qf, kg.astype(jnp.float32)) * scale
        ls = jnp.where(valid[..., None], ls, NEG_INF).reshape(B, H, MOBA_Q_CHUNK, ksel * MOBA_BLOCK)
        p = jax.nn.softmax(jnp.concatenate([ls, lo], axis=-1), axis=-1)
        ps = p[..., :ksel * MOBA_BLOCK].reshape(B, H, MOBA_Q_CHUNK, ksel, MOBA_BLOCK)
        po = p[..., ksel * MOBA_BLOCK:]
        return (jnp.einsum('bhqnk,bhqnkd->bhqd', ps.astype(v.dtype), vg)
                + jnp.einsum('bhqk,bhkd->bhqd', po.astype(v.dtype), v_own))

    o = lax.map(one_chunk, jnp.arange(S // MOBA_Q_CHUNK))
    return o.transpose(1, 2, 0, 3, 4).reshape(B, H, S, dh)


def hybrid_mixer(h, cos, sin, w_in, b_fgt, lam_q1, lam_k1, lam_q2, lam_k2, g_subln,
                 w_br_a, w_br_b, w_br_c, w_out, lam_init):
    B, S, _ = h.shape
    proj = jnp.einsum('bsd,de->bse', h, w_in)
    sizes = [DIFF_QK_WIDTH, DIFF_QK_WIDTH, DIFF_V_WIDTH,
             FOX_WIDTH, FOX_WIDTH, FOX_WIDTH,
             MOBA_WIDTH, MOBA_WIDTH, MOBA_WIDTH,
             N_BRANCH * D_MODEL, N_HEADS_FOX]
    idx = np.cumsum(sizes)[:-1].tolist()
    qa, ka, va, qb, kb, vb, qc, kc, vc, gate_logits, fgt_logits = jnp.split(proj, idx, axis=-1)

    def diff_qk(t):
        t = apply_rope(t.reshape(B, S, 2 * N_HEADS_DIFF, HEAD_DIM), cos, sin)
        return t.reshape(B, S, N_HEADS_DIFF, 2, HEAD_DIM).transpose(0, 2, 3, 1, 4)
    lam = (jnp.exp(jnp.sum(lam_q1.astype(jnp.float32) * lam_k1.astype(jnp.float32)))
           - jnp.exp(jnp.sum(lam_q2.astype(jnp.float32) * lam_k2.astype(jnp.float32))) + lam_init)
    va_h = va.reshape(B, S, N_HEADS_DIFF, 2 * HEAD_DIM).transpose(0, 2, 1, 3)
    oa = diff_attention(diff_qk(qa), diff_qk(ka), va_h, lam, lam_init, g_subln)
    ya = jnp.einsum('bse,ed->bsd', oa.transpose(0, 2, 1, 3).reshape(B, S, DIFF_V_WIDTH), w_br_a)

    def heads(t, n):
        return t.reshape(B, S, n, HEAD_DIM).transpose(0, 2, 1, 3)
    log_f = jax.nn.log_sigmoid(fgt_logits.astype(jnp.float32) + b_fgt.astype(jnp.float32))
    ob = forgetting_attention(heads(qb, N_HEADS_FOX), heads(kb, N_HEADS_FOX),
                              heads(vb, N_HEADS_FOX), log_f.transpose(0, 2, 1))
    yb = jnp.einsum('bse,ed->bsd', ob.transpose(0, 2, 1, 3).reshape(B, S, FOX_WIDTH), w_br_b)

    qc_r = apply_rope(qc.reshape(B, S, N_HEADS_MOBA, HEAD_DIM), cos, sin).transpose(0, 2, 1, 3)
    kc_r = apply_rope(kc.reshape(B, S, N_HEADS_MOBA, HEAD_DIM), cos, sin).transpose(0, 2, 1, 3)
    oc = moba_attention(qc_r, kc_r, heads(vc, N_HEADS_MOBA))
    yc = jnp.einsum('bse,ed->bsd', oc.transpose(0, 2, 1, 3).reshape(B, S, MOBA_WIDTH), w_br_c)

    g = jax.nn.sigmoid(gate_logits.astype(jnp.float32)).astype(h.dtype).reshape(B, S, N_BRANCH, D_MODEL)
    merged = g[:, :, 0] * ya + g[:, :, 1] * yb + g[:, :, 2] * yc
    return jnp.einsum('bsd,de->bse', merged, w_out)


def swiglu(h, w_gate_up, w_down):
    u = jnp.einsum('bsd,df->bsf', h, w_gate_up)
    a, b = jnp.split(u, 2, axis=-1)
    return jnp.einsum('bsf,fd->bsd', jax.nn.silu(a) * b, w_down)


def setup_inputs(seed: int = 0) -> dict:
    key = jax.random.key(seed)
    ks = jax.random.split(key, 24)

    def nrm(k, shape, scale):
        return jax.random.normal(k, shape, jnp.float32) * scale

    def gain(k, n):
        return 1.0 + nrm(k, (DEPTH, n), 0.05)

    return {
        'x': nrm(ks[0], (BATCH, SEQ, D_MODEL), 1.0),
        'c': nrm(ks[1], (BATCH, D_MODEL), 1.0),
        'positions': jnp.broadcast_to(jnp.arange(SEQ, dtype=jnp.int32)[None, :], (BATCH, SEQ)),
        'w_ada': nrm(ks[2], (DEPTH, D_MODEL, 6 * D_MODEL), 0.5 * D_MODEL ** -0.5),
        'b_ada': nrm(ks[3], (DEPTH, 6 * D_MODEL), 0.02),
        'g_pre_mix': gain(ks[4], D_MODEL),
        'g_post_mix': gain(ks[5], D_MODEL),
        'w_in': nrm(ks[6], (DEPTH, D_MODEL, D_IN), D_MODEL ** -0.5),
        'b_fgt': nrm(ks[7], (DEPTH, N_HEADS_FOX), 0.1),
        'lam_q1': nrm(ks[8], (DEPTH, HEAD_DIM), 0.1),
        'lam_k1': nrm(ks[9], (DEPTH, HEAD_DIM), 0.1),
        'lam_q2': nrm(ks[10], (DEPTH, HEAD_DIM), 0.1),
        'lam_k2': nrm(ks[11], (DEPTH, HEAD_DIM), 0.1),
        'g_subln': gain(ks[12], 2 * HEAD_DIM),
        'w_br_a': nrm(ks[13], (DEPTH, DIFF_V_WIDTH, D_MODEL), DIFF_V_WIDTH ** -0.5),
        'w_br_b': nrm(ks[14], (DEPTH, FOX_WIDTH, D_MODEL), FOX_WIDTH ** -0.5),
        'w_br_c': nrm(ks[15], (DEPTH, MOBA_WIDTH, D_MODEL), MOBA_WIDTH ** -0.5),
        'w_out': nrm(ks[16], (DEPTH, D_MODEL, D_MODEL), D_MODEL ** -0.5),
        'g_pre_ffn': gain(ks[17], D_MODEL),
        'g_post_ffn': gain(ks[18], D_MODEL),
        'w_gate_up': nrm(ks[19], (DEPTH, D_MODEL, 2 * D_FF), D_MODEL ** -0.5),
        'w_down': nrm(ks[20], (DEPTH, D_FF, D_MODEL), D_FF ** -0.5),
    }


def reference(x, c, positions, w_ada, b_ada, g_pre_mix, g_post_mix, w_in, b_fgt,
              lam_q1, lam_k1, lam_q2, lam_k2, g_subln, w_br_a, w_br_b, w_br_c, w_out,
              g_pre_ffn, g_post_ffn, w_gate_up, w_down):
    cos, sin = rope_tables(positions)
    c_act = jax.nn.silu(c)
    for l in range(DEPTH):
        lam_init = 0.8 - 0.6 * math.exp(-0.3 * l)
        mod = jnp.einsum('bd,de->be', c_act, w_ada[l]) + b_ada[l]
        sh_m, sc_m, gt_m, sh_f, sc_f, gt_f = [m[:, None, :] for m in jnp.split(mod, 6, axis=-1)]

        h = rmsnorm(x, g_pre_mix[l]) * (1.0 + sc_m) + sh_m
        y = hybrid_mixer(h, cos, sin, w_in[l], b_fgt[l], lam_q1[l], lam_k1[l], lam_q2[l], lam_k2[l],
                         g_subln[l], w_br_a[l], w_br_b[l], w_br_c[l], w_out[l], lam_init)
        x = x + gt_m * rmsnorm(y, g_post_mix[l])

        h = rmsnorm(x, g_pre_ffn[l]) * (1.0 + sc_f) + sh_f
        x = x + gt_f * rmsnorm(swiglu(h, w_gate_up[l], w_down[l]), g_post_ffn[l])
    return x
```

```python
import functools
import math

import jax
import jax.numpy as jnp
from jax import lax
from jax.experimental import pallas as pl
from jax.experimental.pallas import tpu as pltpu

D_MODEL = 1024
BATCH = 16
SEQ = 2048
DEPTH = 2
HEAD_DIM = 64
N_HEADS = 4
QKV_WIDTH = 3072
GATE_WIDTH = 3 * D_MODEL
D_FF = 2816
ROPE_THETA = 10000.0
MOBA_BLOCK = 256
MOBA_TOPK = 3
RMS_EPS = 1e-6
NEG_INF = -1e30

LANES = 128
TOKENS = BATCH * SEQ
TM = 512
TILES_PER_BATCH = SEQ // TM
TQ = 256
NQ = SEQ // TQ
N_KV_BLOCKS = SEQ // MOBA_BLOCK
VMEM_LIMIT = 56 * 1024 * 1024

COL_QA, COL_KA, COL_VA = 0, 4, 8
COL_QB, COL_KB, COL_VB = 12, 14, 16
COL_QC, COL_KC, COL_VC = 18, 20, 22
QKV_SECTIONS = (
    (0, 512, True), (512, 512, True), (1024, 512, False),
    (1536, 256, False), (1792, 256, False), (2048, 256, False),
    (2304, 256, True), (2560, 256, True), (2816, 256, False),
)

F32 = jnp.float32
BF16 = jnp.bfloat16


def _params(**kw):
    return pltpu.CompilerParams(vmem_limit_bytes=VMEM_LIMIT, **kw)


def _resident(shape):
    nd = len(shape)
    return pl.BlockSpec(shape, lambda *_: (0,) * nd, pipeline_mode=pl.Buffered(1))


def _dot_nt(a, b):
    return lax.dot_general(a, b, (((1,), (1,)), ((), ())), preferred_element_type=F32)


def _split3(x):
    hi = x.astype(BF16)
    r1 = x - hi.astype(F32)
    mid = r1.astype(BF16)
    lo = (r1 - mid.astype(F32)).astype(BF16)
    return hi, mid, lo


def _rope_table_kernel(pos_ref, invf_ref, sign_ref, cos_ref, sin_ref):
    ang = pos_ref[...].astype(F32) * invf_ref[...]
    cos_ref[...] = jnp.cos(ang)
    sin_ref[...] = jnp.sin(ang) * sign_ref[...]


def _rope_tables(positions):
    rows = 2048
    inv_freq = 1.0 / (ROPE_THETA ** (jnp.arange(0, HEAD_DIM, 2, dtype=F32) / HEAD_DIM))
    invf = jnp.tile(inv_freq, LANES // (HEAD_DIM // 2))[None, :]
    half = (jnp.arange(LANES) % HEAD_DIM) < HEAD_DIM // 2
    sign = jnp.where(half, -1.0, 1.0).astype(F32)[None, :]
    pos = positions.reshape(TOKENS, 1)
    return pl.pallas_call(
        _rope_table_kernel,
        grid=(TOKENS // rows,),
        in_specs=[pl.BlockSpec((rows, 1), lambda i: (i, 0)),
                  pl.BlockSpec((1, LANES), lambda i: (0, 0)),
                  pl.BlockSpec((1, LANES), lambda i: (0, 0))],
        out_specs=[pl.BlockSpec((rows, LANES), lambda i: (i, 0))] * 2,
        out_shape=[jax.ShapeDtypeStruct((TOKENS, LANES), F32)] * 2,
        compiler_params=_params(),
        name="rope_tables",
    )(pos, invf, sign)


def _ada_kernel(c_ref, w_ref, b_ref, o_ref):
    c = c_ref[...]
    c_act = c * jax.nn.sigmoid(c)
    o_ref[0] = jnp.dot(c_act, w_ref[0], preferred_element_type=F32,
                       precision=lax.Precision.HIGHEST) + b_ref[0]


def _ada_mod(c, w_ada, b_ada):
    tn = 1536
    n = 6 * D_MODEL
    return pl.pallas_call(
        _ada_kernel,
        grid=(DEPTH, n // tn),
        in_specs=[pl.BlockSpec((BATCH, D_MODEL), lambda l, j: (0, 0)),
                  pl.BlockSpec((1, D_MODEL, tn), lambda l, j: (l, 0, j)),
                  pl.BlockSpec((1, 1, tn), lambda l, j: (l, 0, j))],
        out_specs=pl.BlockSpec((1, BATCH, tn), lambda l, j: (l, 0, j)),
        out_shape=jax.ShapeDtypeStruct((DEPTH, BATCH, n), F32),
        compiler_params=_params(),
        name="ada_mod",
    )(c, w_ada, b_ada.reshape(DEPTH, 1, n))


def _modulated_norm(x, g, shift, scale):
    ms = jnp.mean(x * x, axis=-1, keepdims=True)
    return (x * lax.rsqrt(ms + RMS_EPS) * g) * (1.0 + scale) + shift


def _inproj_kernel(x_ref, mod_ref, g_ref, w_ref, wf_ref, bf_ref, cos_ref, sin_ref,
                   qkv_ref, gate_ref, logf_ref):
    h = _modulated_norm(x_ref[...], g_ref[...], mod_ref[0, 0:1, :], mod_ref[0, 1:2, :])
    hb = h.astype(BF16)
    cos = cos_ref[...]
    sin = sin_ref[...]
    lane = lax.broadcasted_iota(jnp.int32, (1, LANES), 1)
    first_half = (lane % HEAD_DIM) < HEAD_DIM // 2

    def rope(t):
        rot = jnp.where(first_half, pltpu.roll(t, LANES - HEAD_DIM // 2, 1),
                        pltpu.roll(t, HEAD_DIM // 2, 1))
        return t * cos + rot * sin

    for c0, width, roped in QKV_SECTIONS:
        acc = jnp.dot(hb, w_ref[:, c0:c0 + width], preferred_element_type=F32)
        if roped:
            for g in range(width // LANES):
                qkv_ref[:, c0 + g * LANES:c0 + (g + 1) * LANES] = rope(
                    acc[:, g * LANES:(g + 1) * LANES]).astype(BF16)
        else:
            qkv_ref[:, c0:c0 + width] = acc.astype(BF16)

    for c0 in range(0, GATE_WIDTH, D_MODEL):
        acc = jnp.dot(hb, w_ref[:, QKV_WIDTH + c0:QKV_WIDTH + c0 + D_MODEL],
                      preferred_element_type=F32)
        gate_ref[:, c0:c0 + D_MODEL] = jax.nn.sigmoid(acc).astype(BF16)

    z = jnp.dot(hb, wf_ref[...], preferred_element_type=F32) + bf_ref[...]
    logf_ref[...] = jnp.minimum(z, 0.0) - jnp.log1p(jnp.exp(-jnp.abs(z)))


def _inproj(x, mod, g_pre, w_main, w_fgt, b_fgt, cos, sin):
    tok = lambda i: (i, 0)
    return pl.pallas_call(
        _inproj_kernel,
        grid=(TOKENS // TM,),
        in_specs=[pl.BlockSpec((TM, D_MODEL), tok),
                  pl.BlockSpec((1, 6, D_MODEL), lambda i: (i // TILES_PER_BATCH, 0, 0)),
                  _resident((1, D_MODEL)),
                  _resident((D_MODEL, QKV_WIDTH + GATE_WIDTH)),
                  _resident((D_MODEL, LANES)),
                  _resident((1, LANES)),
                  pl.BlockSpec((TM, LANES), tok),
                  pl.BlockSpec((TM, LANES), tok)],
        out_specs=[pl.BlockSpec((TM, QKV_WIDTH), tok),
                   pl.BlockSpec((TM, GATE_WIDTH), tok),
                   pl.BlockSpec((TM, LANES), tok)],
        out_shape=[jax.ShapeDtypeStruct((TOKENS, QKV_WIDTH), BF16),
                   jax.ShapeDtypeStruct((TOKENS, GATE_WIDTH), BF16),
                   jax.ShapeDtypeStruct((TOKENS, LANES), F32)],
        compiler_params=_params(),
        name="inproj",
    )(x, mod, g_pre, w_main, w_fgt, b_fgt, cos, sin)


def _prefix_softmax_pv(q, k_ref, v_ref, s_ref, p_ref, i):
    w = (i + 1) * TQ
    s_ref[:, :w] = _dot_nt(q, k_ref[:w, :])
    row = lax.broadcasted_iota(jnp.int32, (TQ, TQ), 0)
    col = lax.broadcasted_iota(jnp.int32, (TQ, TQ), 1)
    s_ref[:, w - TQ:w] = jnp.where(col <= row, s_ref[:, w - TQ:w], NEG_INF)
    n_groups = w // LANES
    mm = s_ref[:, 0:LANES]
    for g in range(1, n_groups):
        mm = jnp.maximum(mm, s_ref[:, g * LANES:(g + 1) * LANES])
    m = jnp.max(mm, axis=-1, keepdims=True)
    ll = jnp.zeros((TQ, LANES), F32)
    for g in range(n_groups):
        p = jnp.exp(s_ref[:, g * LANES:(g + 1) * LANES] - m)
        ll = ll + p
        p_ref[:, g * LANES:(g + 1) * LANES] = p.astype(BF16)
    o = jnp.dot(p_ref[:, :w], v_ref[:w, :], preferred_element_type=F32)
    return o, jnp.sum(ll, axis=-1, keepdims=True)


def _head_masks(shape):
    lane = lax.broadcasted_iota(jnp.int32, shape, 1)
    return lane < HEAD_DIM, lane >= HEAD_DIM


def _attn_call(kernel, extra_inputs, extra_specs, qkv, cols, n_pairs, scratch, name):
    cq, ck, cv = cols
    blk = (None, SEQ, LANES)
    return pl.pallas_call(
        kernel,
        grid=(BATCH, n_pairs),
        in_specs=list(extra_specs) + [
            pl.BlockSpec(blk, lambda b, p: (b, 0, cq + p)),
            pl.BlockSpec(blk, lambda b, p: (b, 0, ck + p)),
            pl.BlockSpec(blk, lambda b, p: (b, 0, cv + p))],
        out_specs=pl.BlockSpec(blk, lambda b, p: (b, 0, p)),
        out_shape=jax.ShapeDtypeStruct((BATCH, SEQ, n_pairs * LANES), BF16),
        scratch_shapes=[pltpu.VMEM((TQ, SEQ), F32), pltpu.VMEM((TQ, SEQ), BF16)] + list(scratch),
        compiler_params=_params(),
        name=name,
    )(*extra_inputs, qkv, qkv, qkv)


def _diff_kernel(lam_ref, gsub_ref, q_ref, k_ref, v_ref, o_ref, s_ref, p_ref, *, lam_init):
    lp = lam_ref[...]
    lam = (jnp.exp(jnp.sum(lp[0:1] * lp[1:2], axis=-1, keepdims=True))
           - jnp.exp(jnp.sum(lp[2:3] * lp[3:4], axis=-1, keepdims=True)) + lam_init)
    first, second = _head_masks((TQ, LANES))
    for i in range(NQ):
        q2 = q_ref[i * TQ:(i + 1) * TQ, :]
        zero = jnp.zeros_like(q2)
        o1, l1 = _prefix_softmax_pv(jnp.where(first, q2, zero), k_ref, v_ref, s_ref, p_ref, i)
        o2, l2 = _prefix_softmax_pv(jnp.where(second, q2, zero), k_ref, v_ref, s_ref, p_ref, i)
        o = o1 / l1 - lam * (o2 / l2)
        ms = jnp.mean(o * o, axis=-1, keepdims=True)
        o = (o * lax.rsqrt(ms + RMS_EPS) * gsub_ref[...]) * (1.0 - lam_init)
        o_ref[i * TQ:(i + 1) * TQ, :] = o.astype(BF16)


def _diff_attention(qkv, lam_params, g_subln, lam_init):
    return _attn_call(
        functools.partial(_diff_kernel, lam_init=lam_init),
        (lam_params, g_subln),
        (pl.BlockSpec((4, HEAD_DIM), lambda b, p: (0, 0)),
         pl.BlockSpec((1, 2 * HEAD_DIM), lambda b, p: (0, 0))),
        qkv, (COL_QA, COL_KA, COL_VA), N_HEADS, (), "diff_attention")


def _fox_kernel(logf_ref, q_ref, k_ref, v_ref, o_ref, s_ref, p_ref, qaug_ref, kaug_ref):
    pair = pl.program_id(1)
    tri = (lax.broadcasted_iota(jnp.int32, (TQ, TQ), 0)
           >= lax.broadcasted_iota(jnp.int32, (TQ, TQ), 1)).astype(BF16)
    lane = lax.broadcasted_iota(jnp.int32, (1, LANES), 1)
    first, second = _head_masks((SEQ, LANES))
    q2 = q_ref[...]
    k2 = k_ref[...]
    zero = jnp.zeros_like(q2)

    carry = jnp.zeros((1, LANES), F32)
    cums = []
    for blk in range(SEQ // TQ):
        hi, mid, lo = _split3(logf_ref[blk * TQ:(blk + 1) * TQ, :])
        r = jnp.dot(tri, jnp.concatenate([hi, mid, lo], axis=1), preferred_element_type=F32)
        cb = (r[:, :LANES] + r[:, LANES:2 * LANES]) + r[:, 2 * LANES:] + carry
        carry = cb[TQ - 1:TQ, :]
        cums.append(cb)
    cum = jnp.concatenate(cums, axis=0)

    for half, mask in enumerate((first, second)):
        head = 2 * pair + half
        c = jnp.sum(jnp.where(lane == head, cum, 0.0), axis=-1, keepdims=True)
        hi, mid, lo = (t.astype(F32) for t in _split3(c))
        eq = jnp.where(lane == 0, hi, jnp.where(lane == 1, mid, jnp.where(
            lane == 2, lo, jnp.where(lane < 6, 1.0, 0.0))))
        ek = jnp.where(lane < 3, 1.0, jnp.where(lane == 3, -hi, jnp.where(
            lane == 4, -mid, jnp.where(lane == 5, -lo, 0.0))))
        qaug_ref[half, :, :LANES] = jnp.where(mask, q2, zero)
        qaug_ref[half, :, LANES:] = eq.astype(BF16)
        kaug_ref[half, :, :LANES] = k2
        kaug_ref[half, :, LANES:] = ek.astype(BF16)

    first_tile, _ = _head_masks((TQ, LANES))
    for i in range(NQ):
        rows = slice(i * TQ, (i + 1) * TQ)
        o1, l1 = _prefix_softmax_pv(qaug_ref[0, rows, :], kaug_ref.at[0], v_ref, s_ref, p_ref, i)
        o2, l2 = _prefix_softmax_pv(qaug_ref[1, rows, :], kaug_ref.at[1], v_ref, s_ref, p_ref, i)
        o_ref[rows, :] = jnp.where(first_tile, o1 / l1, o2 / l2).astype(BF16)


def _fox_attention(qkv, logf):
    return _attn_call(
        _fox_kernel, (logf,),
        (pl.BlockSpec((None, SEQ, LANES), lambda b, p: (b, 0, 0)),),
        qkv, (COL_QB, COL_KB, COL_VB), N_HEADS // 2,
        (pltpu.VMEM((2, SEQ, 2 * LANES), BF16), pltpu.VMEM((2, SEQ, 2 * LANES), BF16)),
        "fox_attention")


def _moba_kernel(q_ref, k_ref, v_ref, o_ref, s_ref, p_ref, kaug_ref):
    k2 = k_ref[...]
    key_row = lax.broadcasted_iota(jnp.int32, (SEQ, LANES), 0)
    key_lane = lax.broadcasted_iota(jnp.int32, (SEQ, LANES), 1)
    kaug_ref[:, :LANES] = k2
    kaug_ref[:, LANES:] = jnp.where(key_row // MOBA_BLOCK == key_lane, 1.0, 0.0).astype(BF16)

    kmean = jnp.sum(k2.astype(F32).reshape(N_KV_BLOCKS, MOBA_BLOCK, LANES), axis=1) * (1.0 / MOBA_BLOCK)
    km_hi = kmean.astype(BF16)
    km_lo = (kmean - km_hi.astype(F32)).astype(BF16)
    kmx = jnp.concatenate([km_hi, km_lo], axis=1)
    kmx = jnp.concatenate([kmx, jnp.zeros((LANES - N_KV_BLOCKS, 2 * LANES), BF16)], axis=0)

    first, second = _head_masks((TQ, LANES))
    lane = lax.broadcasted_iota(jnp.int32, (TQ, LANES), 1)
    lane_f = lane.astype(F32)
    for i in range(NQ):
        rows = slice(i * TQ, (i + 1) * TQ)
        q2 = q_ref[rows, :]
        zero = jnp.zeros_like(q2)
        outs = []
        for mask in (first, second):
            qm = jnp.where(mask, q2, zero)
            if i <= MOBA_TOPK:
                bias = jnp.zeros((TQ, LANES), F32)
            else:
                gate = _dot_nt(jnp.concatenate([qm, qm], axis=1), kmx)
                gate = jnp.where(lane < i, gate, NEG_INF)
                bias = jnp.where(lane < i, NEG_INF, 0.0)
                for _ in range(MOBA_TOPK):
                    top = jnp.max(gate, axis=-1, keepdims=True)
                    idx = jnp.min(jnp.where(gate == top, lane_f, float(LANES)), axis=-1, keepdims=True)
                    hit = lane_f == idx
                    bias = jnp.where(hit, 0.0, bias)
                    gate = jnp.where(hit, NEG_INF, gate)
            qaug = jnp.concatenate([qm, bias.astype(BF16)], axis=1)
            o, l = _prefix_softmax_pv(qaug, kaug_ref, v_ref, s_ref, p_ref, i)
            outs.append(o / l)
        o_ref[rows, :] = jnp.where(first, outs[0], outs[1]).astype(BF16)


def _moba_attention(qkv):
    return _attn_call(
        _moba_kernel, (), (), qkv, (COL_QC, COL_KC, COL_VC), N_HEADS // 2,
        (pltpu.VMEM((SEQ, 2 * LANES), BF16),), "moba_attention")


def _post_norm_residual(x, y, g, gate):
    ms = jnp.mean(y * y, axis=-1, keepdims=True)
    return x + gate * (y * lax.rsqrt(ms + RMS_EPS) * g)


def _merge_kernel(x_ref, mod_ref, g_ref, oa_ref, ob_ref, oc_ref, gate_ref,
                  wa_ref, wb_ref, wc_ref, wo_ref, o_ref):
    merged = None
    for j, (o_br, w_br) in enumerate(((oa_ref, wa_ref), (ob_ref, wb_ref), (oc_ref, wc_ref))):
        y = jnp.dot(o_br[...], w_br[...], preferred_element_type=F32)
        t = gate_ref[:, j * D_MODEL:(j + 1) * D_MODEL].astype(F32) * y
        merged = t if merged is None else merged + t
    y = jnp.dot(merged.astype(BF16), wo_ref[...], preferred_element_type=F32)
    o_ref[...] = _post_norm_residual(x_ref[...], y, g_ref[...], mod_ref[0, 2:3, :])


def _merge(x, mod, g_post, oa, ob, oc, gates, wa, wb, wc, wo):
    tok = lambda i: (i, 0)
    return pl.pallas_call(
        _merge_kernel,
        grid=(TOKENS // TM,),
        in_specs=[pl.BlockSpec((TM, D_MODEL), tok),
                  pl.BlockSpec((1, 6, D_MODEL), lambda i: (i // TILES_PER_BATCH, 0, 0)),
                  _resident((1, D_MODEL)),
                  pl.BlockSpec((TM, oa.shape[1]), tok),
                  pl.BlockSpec((TM, ob.shape[1]), tok),
                  pl.BlockSpec((TM, oc.shape[1]), tok),
                  pl.BlockSpec((TM, GATE_WIDTH), tok),
                  _resident(wa.shape), _resident(wb.shape), _resident(wc.shape),
                  _resident(wo.shape)],
        out_specs=pl.BlockSpec((TM, D_MODEL), tok),
        out_shape=jax.ShapeDtypeStruct((TOKENS, D_MODEL), F32),
        compiler_params=_params(),
        name="merge_out",
    )(x, mod, g_post, oa, ob, oc, gates, wa, wb, wc, wo)


FFN_CHUNKS = ((0, 1536), (1536, D_FF))


def _ffn_kernel(x_ref, mod_ref, gpre_ref, gpost_ref, wgu_ref, wd_ref, o_ref):
    x = x_ref[...]
    hb = _modulated_norm(x, gpre_ref[...], mod_ref[0, 3:4, :], mod_ref[0, 4:5, :]).astype(BF16)
    y = None
    for c0, c1 in FFN_CHUNKS:
        a = jnp.dot(hb, wgu_ref[:, c0:c1], preferred_element_type=F32)
        b = jnp.dot(hb, wgu_ref[:, D_FF + c0:D_FF + c1], preferred_element_type=F32)
        act = (a * jax.nn.sigmoid(a) * b).astype(BF16)
        t = jnp.dot(act, wd_ref[c0:c1, :], preferred_element_type=F32)
        y = t if y is None else y + t
    o_ref[...] = _post_norm_residual(x, y, gpost_ref[...], mod_ref[0, 5:6, :])


def _ffn(x, mod, g_pre, g_post, wgu, wd):
    tok = lambda i: (i, 0)
    return pl.pallas_call(
        _ffn_kernel,
        grid=(TOKENS // TM,),
        in_specs=[pl.BlockSpec((TM, D_MODEL), tok),
                  pl.BlockSpec((1, 6, D_MODEL), lambda i: (i // TILES_PER_BATCH, 0, 0)),
                  _resident((1, D_MODEL)), _resident((1, D_MODEL)),
                  _resident(wgu.shape), _resident(wd.shape)],
        out_specs=pl.BlockSpec((TM, D_MODEL), tok),
        out_shape=jax.ShapeDtypeStruct((TOKENS, D_MODEL), F32),
        compiler_params=_params(),
        name="swiglu",
    )(x, mod, g_pre, g_post, wgu, wd)


def kernel(x, c, positions, w_ada, b_ada, g_pre_mix, g_post_mix, w_in, b_fgt, lam_q1, lam_k1,
           lam_q2, lam_k2, g_subln, w_br_a, w_br_b, w_br_c, w_out, g_pre_ffn, g_post_ffn,
           w_gate_up, w_down):
    cos, sin = _rope_tables(positions)
    mod = _ada_mod(c, w_ada, b_ada).reshape(DEPTH, BATCH, 6, D_MODEL)

    scale = HEAD_DIM ** -0.5
    col = jnp.arange(QKV_WIDTH + GATE_WIDTH)
    is_q = (col < 512) | ((col >= 1536) & (col < 1792)) | ((col >= 2304) & (col < 2560))
    col_scale = jnp.where(is_q, scale, 1.0).astype(F32)

    xt = x.reshape(TOKENS, D_MODEL)
    for l in range(DEPTH):
        lam_init = 0.8 - 0.6 * math.exp(-0.3 * l)
        n_main = QKV_WIDTH + GATE_WIDTH
        w_main = (w_in[l, :, :n_main] * col_scale).astype(BF16)
        w_fgt = jnp.pad(w_in[l, :, n_main:], ((0, 0), (0, LANES - N_HEADS))).astype(BF16)
        bf = jnp.pad(b_fgt[l], (0, LANES - N_HEADS))[None, :]
        qkv, gates, logf = _inproj(xt, mod[l], g_pre_mix[l][None, :], w_main, w_fgt, bf, cos, sin)

        qkv3 = qkv.reshape(BATCH, SEQ, QKV_WIDTH)
        lam_params = jnp.stack([lam_q1[l], lam_k1[l], lam_q2[l], lam_k2[l]])
        oa = _diff_attention(qkv3, lam_params, g_subln[l][None, :], lam_init)
        ob = _fox_attention(qkv3, logf.reshape(BATCH, SEQ, LANES))
        oc = _moba_attention(qkv3)

        xt = _merge(xt, mod[l], g_post_mix[l][None, :],
                    oa.reshape(TOKENS, -1), ob.reshape(TOKENS, -1), oc.reshape(TOKENS, -1), gates,
                    w_br_a[l].astype(BF16), w_br_b[l].astype(BF16), w_br_c[l].astype(BF16),
                    w_out[l].astype(BF16))
        xt = _ffn(xt, mod[l], g_pre_ffn[l][None, :], g_post_ffn[l][None, :],
                  w_gate_up[l].astype(BF16), w_down[l].astype(BF16))
    return xt.reshape(BATCH, SEQ, D_MODEL)
```

```python
import functools
import math

import jax
import jax.numpy as jnp
from jax import lax
from jax.experimental import pallas as pl
from jax.experimental.pallas import tpu as pltpu

D_MODEL = 1024
BATCH = 16
SEQ = 2048
DEPTH = 2
HEAD_DIM = 64
N_HEADS = 4
QKV_WIDTH = 3072
GATE_WIDTH = 3 * D_MODEL
D_FF = 2816
ROPE_THETA = 10000.0
MOBA_BLOCK = 256
MOBA_TOPK = 3
RMS_EPS = 1e-6
NEG_INF = -1e30
LOG2E = math.log2(math.e)

LANES = 128
TOKENS = BATCH * SEQ
TM = 512
TILES_PER_BATCH = SEQ // TM
TQ = 256
NQ = SEQ // TQ
N_KV_BLOCKS = SEQ // MOBA_BLOCK
VMEM_LIMIT = 56 * 1024 * 1024

COL_QA, COL_KA, COL_VA = 0, 4, 8
COL_QB, COL_KB, COL_VB = 12, 14, 16
COL_QC, COL_KC, COL_VC = 18, 20, 22
QKV_SECTIONS = (
    (0, 512, True), (512, 512, True), (1024, 512, False),
    (1536, 256, False), (1792, 256, False), (2048, 256, False),
    (2304, 256, True), (2560, 256, True), (2816, 256, False),
)

F32 = jnp.float32
BF16 = jnp.bfloat16


def _params(**kw):
    return pltpu.CompilerParams(vmem_limit_bytes=VMEM_LIMIT, **kw)


def _resident(shape):
    nd = len(shape)
    return pl.BlockSpec(shape, lambda *_: (0,) * nd, pipeline_mode=pl.Buffered(1))


def _dot_nt(a, b):
    return lax.dot_general(a, b, (((1,), (1,)), ((), ())), preferred_element_type=F32)


def _split3(x):
    hi = x.astype(BF16)
    r1 = x - hi.astype(F32)
    mid = r1.astype(BF16)
    lo = (r1 - mid.astype(F32)).astype(BF16)
    return hi, mid, lo


def _rope_table_kernel(pos_ref, invf_ref, sign_ref, cos_ref, sin_ref):
    ang = pos_ref[...].astype(F32) * invf_ref[...]
    cos_ref[...] = jnp.cos(ang)
    sin_ref[...] = jnp.sin(ang) * sign_ref[...]


def _rope_tables(positions):
    rows = 2048
    inv_freq = 1.0 / (ROPE_THETA ** (jnp.arange(0, HEAD_DIM, 2, dtype=F32) / HEAD_DIM))
    invf = jnp.tile(inv_freq, LANES // (HEAD_DIM // 2))[None, :]
    half = (jnp.arange(LANES) % HEAD_DIM) < HEAD_DIM // 2
    sign = jnp.where(half, -1.0, 1.0).astype(F32)[None, :]
    pos = positions.reshape(TOKENS, 1)
    return pl.pallas_call(
        _rope_table_kernel,
        grid=(TOKENS // rows,),
        in_specs=[pl.BlockSpec((rows, 1), lambda i: (i, 0)),
                  pl.BlockSpec((1, LANES), lambda i: (0, 0)),
                  pl.BlockSpec((1, LANES), lambda i: (0, 0))],
        out_specs=[pl.BlockSpec((rows, LANES), lambda i: (i, 0))] * 2,
        out_shape=[jax.ShapeDtypeStruct((TOKENS, LANES), F32)] * 2,
        compiler_params=_params(),
        name="rope_tables",
    )(pos, invf, sign)


def _ada_kernel(c_ref, w_ref, b_ref, o_ref):
    c = c_ref[...]
    c_act = c * jax.nn.sigmoid(c)
    o_ref[0] = jnp.dot(c_act, w_ref[0], preferred_element_type=F32,
                       precision=lax.Precision.HIGHEST) + b_ref[0]


def _ada_mod(c, w_ada, b_ada):
    tn = 1536
    n = 6 * D_MODEL
    return pl.pallas_call(
        _ada_kernel,
        grid=(DEPTH, n // tn),
        in_specs=[pl.BlockSpec((BATCH, D_MODEL), lambda l, j: (0, 0)),
                  pl.BlockSpec((1, D_MODEL, tn), lambda l, j: (l, 0, j)),
                  pl.BlockSpec((1, 1, tn), lambda l, j: (l, 0, j))],
        out_specs=pl.BlockSpec((1, BATCH, tn), lambda l, j: (l, 0, j)),
        out_shape=jax.ShapeDtypeStruct((DEPTH, BATCH, n), F32),
        compiler_params=_params(),
        name="ada_mod",
    )(c, w_ada, b_ada.reshape(DEPTH, 1, n))


def _modulated_norm(x, g, shift, scale):
    ms = jnp.mean(x * x, axis=-1, keepdims=True)
    return (x * lax.rsqrt(ms + RMS_EPS) * g) * (1.0 + scale) + shift


def _inproj_kernel(x_ref, mod_ref, g_ref, w_ref, wf_ref, bf_ref, cos_ref, sin_ref,
                   qkv_ref, gate_ref, logf_ref):
    h = _modulated_norm(x_ref[...], g_ref[...], mod_ref[0, 0:1, :], mod_ref[0, 1:2, :])
    hb = h.astype(BF16)
    cos = cos_ref[...]
    sin = sin_ref[...]
    lane = lax.broadcasted_iota(jnp.int32, (1, LANES), 1)
    first_half = (lane % HEAD_DIM) < HEAD_DIM // 2

    def rope(t):
        rot = jnp.where(first_half, pltpu.roll(t, LANES - HEAD_DIM // 2, 1),
                        pltpu.roll(t, HEAD_DIM // 2, 1))
        return t * cos + rot * sin

    for c0, width, roped in QKV_SECTIONS:
        acc = jnp.dot(hb, w_ref[:, c0:c0 + width], preferred_element_type=F32)
        if roped:
            for g in range(width // LANES):
                qkv_ref[:, c0 + g * LANES:c0 + (g + 1) * LANES] = rope(
                    acc[:, g * LANES:(g + 1) * LANES]).astype(BF16)
        else:
            qkv_ref[:, c0:c0 + width] = acc.astype(BF16)

    for c0 in range(0, GATE_WIDTH, D_MODEL):
        acc = jnp.dot(hb, w_ref[:, QKV_WIDTH + c0:QKV_WIDTH + c0 + D_MODEL],
                      preferred_element_type=F32)
        gate_ref[:, c0:c0 + D_MODEL] = jax.nn.sigmoid(acc).astype(BF16)

    z = jnp.dot(hb, wf_ref[...], preferred_element_type=F32) + bf_ref[...]
    logf_ref[...] = jnp.minimum(z, 0.0) - jnp.log1p(jnp.exp(-jnp.abs(z)))


def _inproj(x, mod, g_pre, w_main, w_fgt, b_fgt, cos, sin):
    tok = lambda i: (i, 0)
    return pl.pallas_call(
        _inproj_kernel,
        grid=(TOKENS // TM,),
        in_specs=[pl.BlockSpec((TM, D_MODEL), tok),
                  pl.BlockSpec((1, 6, D_MODEL), lambda i: (i // TILES_PER_BATCH, 0, 0)),
                  _resident((1, D_MODEL)),
                  _resident((D_MODEL, QKV_WIDTH + GATE_WIDTH)),
                  _resident((D_MODEL, LANES)),
                  _resident((1, LANES)),
                  pl.BlockSpec((TM, LANES), tok),
                  pl.BlockSpec((TM, LANES), tok)],
        out_specs=[pl.BlockSpec((TM, QKV_WIDTH), tok),
                   pl.BlockSpec((TM, GATE_WIDTH), tok),
                   pl.BlockSpec((TM, LANES), tok)],
        out_shape=[jax.ShapeDtypeStruct((TOKENS, QKV_WIDTH), BF16),
                   jax.ShapeDtypeStruct((TOKENS, GATE_WIDTH), BF16),
                   jax.ShapeDtypeStruct((TOKENS, LANES), F32)],
        compiler_params=_params(),
        name="inproj",
    )(x, mod, g_pre, w_main, w_fgt, b_fgt, cos, sin)


def _scores(q, k_ref, s_ref, i):
    w = (i + 1) * TQ
    s = _dot_nt(q, k_ref[:w, :])
    row = lax.broadcasted_iota(jnp.int32, (TQ, TQ), 0)
    col = lax.broadcasted_iota(jnp.int32, (TQ, TQ), 1)
    if i > 0:
        s_ref[:, :w - TQ] = s[:, :w - TQ]
    s_ref[:, w - TQ:w] = jnp.where(col <= row, s[:, w - TQ:], NEG_INF)


def _softmax_pv(s_ref, p_ref, vaug_ref, i):
    w = (i + 1) * TQ
    n_groups = w // LANES
    mm = s_ref[:, 0:LANES]
    for g in range(1, n_groups):
        mm = jnp.maximum(mm, s_ref[:, g * LANES:(g + 1) * LANES])
    m = jnp.max(mm, axis=-1, keepdims=True)
    for g in range(n_groups):
        p = jnp.exp2(s_ref[:, g * LANES:(g + 1) * LANES] - m)
        p_ref[:, g * LANES:(g + 1) * LANES] = p.astype(BF16)
    return jnp.dot(p_ref[:, :w], vaug_ref[:w, :], preferred_element_type=F32)


def _normalised(o_aug):
    return o_aug[:, :LANES] / o_aug[:, LANES:]


def _attend_pairs(make_q, k_refs, vaug_ref, s_ref, p_ref, finish):
    units = [(i, half) for i in range(NQ) for half in (0, 1)]
    _scores(make_q(0, 0), k_refs[0], s_ref.at[0], 0)
    outs = [None, None]
    for n, (i, half) in enumerate(units):
        if n + 1 < len(units):
            ni, nh = units[n + 1]
            _scores(make_q(ni, nh), k_refs[nh], s_ref.at[nh], ni)
        outs[half] = _normalised(_softmax_pv(s_ref.at[half], p_ref.at[half], vaug_ref, i))
        if half == 1:
            finish(i, outs[0], outs[1])


def _fill_vaug(vaug_ref, v_ref):
    vaug_ref[:, :LANES] = v_ref[...]
    vaug_ref[:, LANES:] = jnp.ones((SEQ, LANES), BF16)


def _head_masks(shape):
    lane = lax.broadcasted_iota(jnp.int32, shape, 1)
    return lane < HEAD_DIM, lane >= HEAD_DIM


def _attn_call(kernel, extra_inputs, extra_specs, qkv, cols, n_pairs, scratch, name):
    cq, ck, cv = cols
    blk = (None, SEQ, LANES)
    return pl.pallas_call(
        kernel,
        grid=(BATCH, n_pairs),
        in_specs=list(extra_specs) + [
            pl.BlockSpec(blk, lambda b, p: (b, 0, cq + p)),
            pl.BlockSpec(blk, lambda b, p: (b, 0, ck + p)),
            pl.BlockSpec(blk, lambda b, p: (b, 0, cv + p))],
        out_specs=pl.BlockSpec(blk, lambda b, p: (b, 0, p)),
        out_shape=jax.ShapeDtypeStruct((BATCH, SEQ, n_pairs * LANES), BF16),
        scratch_shapes=[pltpu.VMEM((2, TQ, SEQ), F32), pltpu.VMEM((2, TQ, SEQ), BF16),
                        pltpu.VMEM((SEQ, 2 * LANES), BF16)] + list(scratch),
        compiler_params=_params(),
        name=name,
    )(*extra_inputs, qkv, qkv, qkv)


def _diff_kernel(lam_ref, gsub_ref, q_ref, k_ref, v_ref, o_ref, s_ref, p_ref, vaug_ref, *, lam_init):
    lp = lam_ref[...]
    lam = (jnp.exp(jnp.sum(lp[0:1] * lp[1:2], axis=-1, keepdims=True))
           - jnp.exp(jnp.sum(lp[2:3] * lp[3:4], axis=-1, keepdims=True)) + lam_init)
    masks = _head_masks((TQ, LANES))
    _fill_vaug(vaug_ref, v_ref)

    def make_q(i, half):
        q2 = q_ref[i * TQ:(i + 1) * TQ, :]
        return jnp.where(masks[half], q2, jnp.zeros_like(q2))

    def finish(i, o1, o2):
        o = o1 - lam * o2
        ms = jnp.mean(o * o, axis=-1, keepdims=True)
        o = (o * lax.rsqrt(ms + RMS_EPS) * gsub_ref[...]) * (1.0 - lam_init)
        o_ref[i * TQ:(i + 1) * TQ, :] = o.astype(BF16)

    _attend_pairs(make_q, (k_ref, k_ref), vaug_ref, s_ref, p_ref, finish)


def _diff_attention(qkv, lam_params, g_subln, lam_init):
    return _attn_call(
        functools.partial(_diff_kernel, lam_init=lam_init),
        (lam_params, g_subln),
        (pl.BlockSpec((4, HEAD_DIM), lambda b, p: (0, 0)),
         pl.BlockSpec((1, 2 * HEAD_DIM), lambda b, p: (0, 0))),
        qkv, (COL_QA, COL_KA, COL_VA), N_HEADS, (), "diff_attention")


def _fox_kernel(logf_ref, q_ref, k_ref, v_ref, o_ref, s_ref, p_ref, vaug_ref, qaug_ref, kaug_ref):
    pair = pl.program_id(1)
    _fill_vaug(vaug_ref, v_ref)
    tri = (lax.broadcasted_iota(jnp.int32, (TQ, TQ), 0)
           >= lax.broadcasted_iota(jnp.int32, (TQ, TQ), 1)).astype(BF16)
    lane = lax.broadcasted_iota(jnp.int32, (1, LANES), 1)
    first, second = _head_masks((SEQ, LANES))
    q2 = q_ref[...]
    k2 = k_ref[...]
    zero = jnp.zeros_like(q2)

    carry = jnp.zeros((1, LANES), F32)
    cums = []
    for blk in range(SEQ // TQ):
        hi, mid, lo = _split3(logf_ref[blk * TQ:(blk + 1) * TQ, :])
        r = jnp.dot(tri, jnp.concatenate([hi, mid, lo], axis=1), preferred_element_type=F32)
        cb = (r[:, :LANES] + r[:, LANES:2 * LANES]) + r[:, 2 * LANES:] + carry
        carry = cb[TQ - 1:TQ, :]
        cums.append(cb)
    cum = jnp.concatenate(cums, axis=0)

    for half, mask in enumerate((first, second)):
        head = 2 * pair + half
        c = jnp.sum(jnp.where(lane == head, cum, 0.0), axis=-1, keepdims=True)
        hi, mid, lo = (t.astype(F32) for t in _split3(c * LOG2E))
        eq = jnp.where(lane == 0, hi, jnp.where(lane == 1, mid, jnp.where(
            lane == 2, lo, jnp.where(lane < 6, 1.0, 0.0))))
        ek = jnp.where(lane < 3, 1.0, jnp.where(lane == 3, -hi, jnp.where(
            lane == 4, -mid, jnp.where(lane == 5, -lo, 0.0))))
        qaug_ref[half, :, :LANES] = jnp.where(mask, q2, zero)
        qaug_ref[half, :, LANES:] = eq.astype(BF16)
        kaug_ref[half, :, :LANES] = k2
        kaug_ref[half, :, LANES:] = ek.astype(BF16)

    first_tile, _ = _head_masks((TQ, LANES))

    def make_q(i, half):
        return qaug_ref[half, i * TQ:(i + 1) * TQ, :]

    def finish(i, o1, o2):
        o_ref[i * TQ:(i + 1) * TQ, :] = jnp.where(first_tile, o1, o2).astype(BF16)

    _attend_pairs(make_q, (kaug_ref.at[0], kaug_ref.at[1]), vaug_ref, s_ref, p_ref, finish)


def _fox_attention(qkv, logf):
    return _attn_call(
        _fox_kernel, (logf,),
        (pl.BlockSpec((None, SEQ, LANES), lambda b, p: (b, 0, 0)),),
        qkv, (COL_QB, COL_KB, COL_VB), N_HEADS // 2,
        (pltpu.VMEM((2, SEQ, 2 * LANES), BF16), pltpu.VMEM((2, SEQ, 2 * LANES), BF16)),
        "fox_attention")


def _moba_kernel(q_ref, k_ref, v_ref, o_ref, s_ref, p_ref, vaug_ref, kaug_ref, qaug_ref):
    _fill_vaug(vaug_ref, v_ref)
    k2 = k_ref[...]
    key_row = lax.broadcasted_iota(jnp.int32, (SEQ, LANES), 0)
    key_lane = lax.broadcasted_iota(jnp.int32, (SEQ, LANES), 1)
    kaug_ref[:, :LANES] = k2
    kaug_ref[:, LANES:] = jnp.where(key_row // MOBA_BLOCK == key_lane, 1.0, 0.0).astype(BF16)

    kmean = jnp.sum(k2.astype(F32).reshape(N_KV_BLOCKS, MOBA_BLOCK, LANES), axis=1) * (1.0 / MOBA_BLOCK)
    km_hi = kmean.astype(BF16)
    km_lo = (kmean - km_hi.astype(F32)).astype(BF16)
    kmx = jnp.concatenate([km_hi, km_lo], axis=1)
    kmx = jnp.concatenate([kmx, jnp.zeros((LANES - N_KV_BLOCKS, 2 * LANES), BF16)], axis=0)

    n_easy = (MOBA_TOPK + 1) * MOBA_BLOCK
    n_hard = SEQ - n_easy
    q2 = q_ref[...]
    lane = lax.broadcasted_iota(jnp.int32, (n_hard, LANES), 1)
    lane_f = lane.astype(F32)
    own = lax.broadcasted_iota(jnp.int32, (n_hard, LANES), 0) // MOBA_BLOCK + (MOBA_TOPK + 1)
    past = lane < own
    for half, mask in enumerate(_head_masks((SEQ, LANES))):
        qm = jnp.where(mask, q2, jnp.zeros_like(q2))
        qh = qm[n_easy:, :]
        gate = _dot_nt(jnp.concatenate([qh, qh], axis=1), kmx)
        gate = jnp.where(past, gate, NEG_INF)
        bias = jnp.where(past, NEG_INF, 0.0)
        for _ in range(MOBA_TOPK):
            top = jnp.max(gate, axis=-1, keepdims=True)
            idx = jnp.min(jnp.where(gate == top, lane_f, float(LANES)), axis=-1, keepdims=True)
            hit = lane_f == idx
            bias = jnp.where(hit, 0.0, bias)
            gate = jnp.where(hit, NEG_INF, gate)
        qaug_ref[half, :, :LANES] = qm
        qaug_ref[half, :n_easy, LANES:] = jnp.zeros((n_easy, LANES), BF16)
        qaug_ref[half, n_easy:, LANES:] = bias.astype(BF16)

    first_tile, _ = _head_masks((TQ, LANES))

    def make_q(i, half):
        return qaug_ref[half, i * TQ:(i + 1) * TQ, :]

    def finish(i, o1, o2):
        o_ref[i * TQ:(i + 1) * TQ, :] = jnp.where(first_tile, o1, o2).astype(BF16)

    _attend_pairs(make_q, (kaug_ref, kaug_ref), vaug_ref, s_ref, p_ref, finish)


def _moba_attention(qkv):
    return _attn_call(
        _moba_kernel, (), (), qkv, (COL_QC, COL_KC, COL_VC), N_HEADS // 2,
        (pltpu.VMEM((SEQ, 2 * LANES), BF16), pltpu.VMEM((2, SEQ, 2 * LANES), BF16)),
        "moba_attention")


def _post_norm_residual(x, y, g, gate):
    ms = jnp.mean(y * y, axis=-1, keepdims=True)
    return x + gate * (y * lax.rsqrt(ms + RMS_EPS) * g)


def _merge_kernel(x_ref, mod_ref, g_ref, oa_ref, ob_ref, oc_ref, gate_ref,
                  wa_ref, wb_ref, wc_ref, wo_ref, o_ref):
    merged = None
    for j, (o_br, w_br) in enumerate(((oa_ref, wa_ref), (ob_ref, wb_ref), (oc_ref, wc_ref))):
        y = jnp.dot(o_br[...], w_br[...], preferred_element_type=F32)
        t = gate_ref[:, j * D_MODEL:(j + 1) * D_MODEL].astype(F32) * y
        merged = t if merged is None else merged + t
    y = jnp.dot(merged.astype(BF16), wo_ref[...], preferred_element_type=F32)
    o_ref[...] = _post_norm_residual(x_ref[...], y, g_ref[...], mod_ref[0, 2:3, :])


def _merge(x, mod, g_post, oa, ob, oc, gates, wa, wb, wc, wo):
    tok = lambda i: (i, 0)
    return pl.pallas_call(
        _merge_kernel,
        grid=(TOKENS // TM,),
        in_specs=[pl.BlockSpec((TM, D_MODEL), tok),
                  pl.BlockSpec((1, 6, D_MODEL), lambda i: (i // TILES_PER_BATCH, 0, 0)),
                  _resident((1, D_MODEL)),
                  pl.BlockSpec((TM, oa.shape[1]), tok),
                  pl.BlockSpec((TM, ob.shape[1]), tok),
                  pl.BlockSpec((TM, oc.shape[1]), tok),
                  pl.BlockSpec((TM, GATE_WIDTH), tok),
                  _resident(wa.shape), _resident(wb.shape), _resident(wc.shape),
                  _resident(wo.shape)],
        out_specs=pl.BlockSpec((TM, D_MODEL), tok),
        out_shape=jax.ShapeDtypeStruct((TOKENS, D_MODEL), F32),
        compiler_params=_params(),
        name="merge_out",
    )(x, mod, g_post, oa, ob, oc, gates, wa, wb, wc, wo)


FFN_CHUNKS = ((0, 1536), (1536, D_FF))


def _ffn_kernel(x_ref, mod_ref, gpre_ref, gpost_ref, wgu_ref, wd_ref, o_ref):
    x = x_ref[...]
    hb = _modulated_norm(x, gpre_ref[...], mod_ref[0, 3:4, :], mod_ref[0, 4:5, :]).astype(BF16)
    y = None
    for c0, c1 in FFN_CHUNKS:
        a = jnp.dot(hb, wgu_ref[:, c0:c1], preferred_element_type=F32)
        b = jnp.dot(hb, wgu_ref[:, D_FF + c0:D_FF + c1], preferred_element_type=F32)
        act = (a * jax.nn.sigmoid(a) * b).astype(BF16)
        t = jnp.dot(act, wd_ref[c0:c1, :], preferred_element_type=F32)
        y = t if y is None else y + t
    o_ref[...] = _post_norm_residual(x, y, gpost_ref[...], mod_ref[0, 5:6, :])


def _ffn(x, mod, g_pre, g_post, wgu, wd):
    tok = lambda i: (i, 0)
    return pl.pallas_call(
        _ffn_kernel,
        grid=(TOKENS // TM,),
        in_specs=[pl.BlockSpec((TM, D_MODEL), tok),
                  pl.BlockSpec((1, 6, D_MODEL), lambda i: (i // TILES_PER_BATCH, 0, 0)),
                  _resident((1, D_MODEL)), _resident((1, D_MODEL)),
                  _resident(wgu.shape), _resident(wd.shape)],
        out_specs=pl.BlockSpec((TM, D_MODEL), tok),
        out_shape=jax.ShapeDtypeStruct((TOKENS, D_MODEL), F32),
        compiler_params=_params(),
        name="swiglu",
    )(x, mod, g_pre, g_post, wgu, wd)


def kernel(x, c, positions, w_ada, b_ada, g_pre_mix, g_post_mix, w_in, b_fgt, lam_q1, lam_k1,
           lam_q2, lam_k2, g_subln, w_br_a, w_br_b, w_br_c, w_out, g_pre_ffn, g_post_ffn,
           w_gate_up, w_down):
    cos, sin = _rope_tables(positions)
    mod = _ada_mod(c, w_ada, b_ada).reshape(DEPTH, BATCH, 6, D_MODEL)

    scale = HEAD_DIM ** -0.5 * LOG2E
    col = jnp.arange(QKV_WIDTH + GATE_WIDTH)
    is_q = (col < 512) | ((col >= 1536) & (col < 1792)) | ((col >= 2304) & (col < 2560))
    col_scale = jnp.where(is_q, scale, 1.0).astype(F32)

    xt = x.reshape(TOKENS, D_MODEL)
    for l in range(DEPTH):
        lam_init = 0.8 - 0.6 * math.exp(-0.3 * l)
        n_main = QKV_WIDTH + GATE_WIDTH
        w_main = (w_in[l, :, :n_main] * col_scale).astype(BF16)
        w_fgt = jnp.pad(w_in[l, :, n_main:], ((0, 0), (0, LANES - N_HEADS))).astype(BF16)
        bf = jnp.pad(b_fgt[l], (0, LANES - N_HEADS))[None, :]
        qkv, gates, logf = _inproj(xt, mod[l], g_pre_mix[l][None, :], w_main, w_fgt, bf, cos, sin)

        qkv3 = qkv.reshape(BATCH, SEQ, QKV_WIDTH)
        lam_params = jnp.stack([lam_q1[l], lam_k1[l], lam_q2[l], lam_k2[l]])
        oa = _diff_attention(qkv3, lam_params, g_subln[l][None, :], lam_init)
        ob = _fox_attention(qkv3, logf.reshape(BATCH, SEQ, LANES))
        oc = _moba_attention(qkv3)

        xt = _merge(xt, mod[l], g_post_mix[l][None, :],
                    oa.reshape(TOKENS, -1), ob.reshape(TOKENS, -1), oc.reshape(TOKENS, -1), gates,
                    w_br_a[l].astype(BF16), w_br_b[l].astype(BF16), w_br_c[l].astype(BF16),
                    w_out[l].astype(BF16))
        xt = _ffn(xt, mod[l], g_pre_ffn[l][None, :], g_post_ffn[l][None, :],
                  w_gate_up[l].astype(BF16), w_down[l].astype(BF16))
    return xt.reshape(BATCH, SEQ, D_MODEL)
```

```python
import functools
import math

import jax
import jax.numpy as jnp
from jax import lax
from jax.experimental import pallas as pl
from jax.experimental.pallas import tpu as pltpu

D_MODEL = 1024
BATCH = 16
SEQ = 2048
DEPTH = 2
HEAD_DIM = 64
N_HEADS = 4
QKV_WIDTH = 3072
GATE_WIDTH = 3 * D_MODEL
D_FF = 2816
ROPE_THETA = 10000.0
MOBA_BLOCK = 256
MOBA_TOPK = 3
RMS_EPS = 1e-6
NEG_INF = -1e30
LOG2E = math.log2(math.e)

LANES = 128
TOKENS = BATCH * SEQ
TM = 512
TILES_PER_BATCH = SEQ // TM
SUB = 256
TQ = 256
NQ = SEQ // TQ
N_KV_BLOCKS = SEQ // MOBA_BLOCK
VMEM_LIMIT = 56 * 1024 * 1024

COL_QA, COL_KA, COL_VA = 0, 4, 8
COL_QB, COL_KB, COL_VB = 12, 14, 16
COL_QC, COL_KC, COL_VC = 18, 20, 22
QKV_SECTIONS = (
    (0, 512, True), (512, 512, True), (1024, 512, False),
    (1536, 256, False), (1792, 256, False), (2048, 256, False),
    (2304, 256, True), (2560, 256, True), (2816, 256, False),
)

F32 = jnp.float32
BF16 = jnp.bfloat16


def _params(**kw):
    return pltpu.CompilerParams(vmem_limit_bytes=VMEM_LIMIT, **kw)


def _resident(shape):
    nd = len(shape)
    return pl.BlockSpec(shape, lambda *_: (0,) * nd, pipeline_mode=pl.Buffered(1))


def _dot_nt(a, b):
    return lax.dot_general(a, b, (((1,), (1,)), ((), ())), preferred_element_type=F32)


def _split3(x):
    hi = x.astype(BF16)
    r1 = x - hi.astype(F32)
    mid = r1.astype(BF16)
    lo = (r1 - mid.astype(F32)).astype(BF16)
    return hi, mid, lo


def _rope_table_kernel(pos_ref, invf_ref, sign_ref, cos_ref, sin_ref):
    ang = pos_ref[...].astype(F32) * invf_ref[...]
    cos_ref[...] = jnp.cos(ang)
    sin_ref[...] = jnp.sin(ang) * sign_ref[...]


def _rope_tables(positions):
    rows = 2048
    inv_freq = 1.0 / (ROPE_THETA ** (jnp.arange(0, HEAD_DIM, 2, dtype=F32) / HEAD_DIM))
    invf = jnp.tile(inv_freq, LANES // (HEAD_DIM // 2))[None, :]
    half = (jnp.arange(LANES) % HEAD_DIM) < HEAD_DIM // 2
    sign = jnp.where(half, -1.0, 1.0).astype(F32)[None, :]
    pos = positions.reshape(TOKENS, 1)
    return pl.pallas_call(
        _rope_table_kernel,
        grid=(TOKENS // rows,),
        in_specs=[pl.BlockSpec((rows, 1), lambda i: (i, 0)),
                  pl.BlockSpec((1, LANES), lambda i: (0, 0)),
                  pl.BlockSpec((1, LANES), lambda i: (0, 0))],
        out_specs=[pl.BlockSpec((rows, LANES), lambda i: (i, 0))] * 2,
        out_shape=[jax.ShapeDtypeStruct((TOKENS, LANES), F32)] * 2,
        compiler_params=_params(),
        name="rope_tables",
    )(pos, invf, sign)


def _ada_kernel(c_ref, w_ref, b_ref, o_ref):
    c = c_ref[...]
    a = _split3(c * jax.nn.sigmoid(c))
    w = _split3(w_ref[0])
    acc = b_ref[0]
    for i, j in ((2, 0), (1, 1), (0, 2), (1, 0), (0, 1), (0, 0)):
        acc = acc + jnp.dot(a[i], w[j], preferred_element_type=F32)
    o_ref[0] = acc


def _ada_mod(c, w_ada, b_ada):
    tn = 1536
    n = 6 * D_MODEL
    return pl.pallas_call(
        _ada_kernel,
        grid=(DEPTH, n // tn),
        in_specs=[pl.BlockSpec((BATCH, D_MODEL), lambda l, j: (0, 0)),
                  pl.BlockSpec((1, D_MODEL, tn), lambda l, j: (l, 0, j)),
                  pl.BlockSpec((1, 1, tn), lambda l, j: (l, 0, j))],
        out_specs=pl.BlockSpec((1, BATCH, tn), lambda l, j: (l, 0, j)),
        out_shape=jax.ShapeDtypeStruct((DEPTH, BATCH, n), F32),
        compiler_params=_params(),
        name="ada_mod",
    )(c, w_ada, b_ada.reshape(DEPTH, 1, n))


def _modulated_norm(x, g, shift, scale):
    ms = jnp.mean(x * x, axis=-1, keepdims=True)
    return (x * lax.rsqrt(ms + RMS_EPS) * g) * (1.0 + scale) + shift


def _inproj_kernel(x_ref, mod_ref, g_ref, w_ref, wf_ref, bf_ref, cos_ref, sin_ref,
                   qkv_ref, gate_ref, logf_ref):
    h = _modulated_norm(x_ref[...], g_ref[...], mod_ref[0, 0:1, :], mod_ref[0, 1:2, :])
    hb = h.astype(BF16)
    cos = cos_ref[...]
    sin = sin_ref[...]
    lane = lax.broadcasted_iota(jnp.int32, (1, LANES), 1)
    first_half = (lane % HEAD_DIM) < HEAD_DIM // 2

    def rope(t):
        rot = jnp.where(first_half, pltpu.roll(t, LANES - HEAD_DIM // 2, 1),
                        pltpu.roll(t, HEAD_DIM // 2, 1))
        return t * cos + rot * sin

    for c0, width, roped in QKV_SECTIONS:
        acc = jnp.dot(hb, w_ref[:, c0:c0 + width], preferred_element_type=F32)
        if roped:
            for g in range(width // LANES):
                qkv_ref[:, c0 + g * LANES:c0 + (g + 1) * LANES] = rope(
                    acc[:, g * LANES:(g + 1) * LANES]).astype(BF16)
        else:
            qkv_ref[:, c0:c0 + width] = acc.astype(BF16)

    for c0 in range(0, GATE_WIDTH, D_MODEL):
        acc = jnp.dot(hb, w_ref[:, QKV_WIDTH + c0:QKV_WIDTH + c0 + D_MODEL],
                      preferred_element_type=F32)
        gate_ref[:, c0:c0 + D_MODEL] = jax.nn.sigmoid(acc).astype(BF16)

    z = jnp.dot(hb, wf_ref[...], preferred_element_type=F32) + bf_ref[...]
    logf_ref[...] = jnp.minimum(z, 0.0) - jnp.log1p(jnp.exp(-jnp.abs(z)))


def _inproj(x, mod, g_pre, w_main, w_fgt, b_fgt, cos, sin):
    tok = lambda i: (i, 0)
    return pl.pallas_call(
        _inproj_kernel,
        grid=(TOKENS // TM,),
        in_specs=[pl.BlockSpec((TM, D_MODEL), tok),
                  pl.BlockSpec((1, 6, D_MODEL), lambda i: (i // TILES_PER_BATCH, 0, 0)),
                  _resident((1, D_MODEL)),
                  _resident((D_MODEL, QKV_WIDTH + GATE_WIDTH)),
                  _resident((D_MODEL, LANES)),
                  _resident((1, LANES)),
                  pl.BlockSpec((TM, LANES), tok),
                  pl.BlockSpec((TM, LANES), tok)],
        out_specs=[pl.BlockSpec((TM, QKV_WIDTH), tok),
                   pl.BlockSpec((TM, GATE_WIDTH), tok),
                   pl.BlockSpec((TM, LANES), tok)],
        out_shape=[jax.ShapeDtypeStruct((TOKENS, QKV_WIDTH), BF16),
                   jax.ShapeDtypeStruct((TOKENS, GATE_WIDTH), BF16),
                   jax.ShapeDtypeStruct((TOKENS, LANES), F32)],
        compiler_params=_params(),
        name="inproj",
    )(x, mod, g_pre, w_main, w_fgt, b_fgt, cos, sin)


FAST_ROWS = 128
F32_HUGE = 3.0e38


def _causal(s, row_offset):
    row = lax.broadcasted_iota(jnp.int32, s.shape, 0) + row_offset
    col = lax.broadcasted_iota(jnp.int32, s.shape, 1)
    return jnp.where(col <= row, s, NEG_INF)


def _fast_unit(q, diag_bias, bias, k_ref, vaug_ref, i, row0):
    r = q.shape[0]
    m = jnp.sum(q.astype(F32) * k_ref[row0:row0 + r, :].astype(F32), axis=-1, keepdims=True)
    if diag_bias is not None:
        m = m + diag_bias
    acc = None
    for c in range(i + 1):
        s = _dot_nt(q, k_ref[c * TQ:(c + 1) * TQ, :])
        shift = m
        b = bias(c)
        if b is not None and b.shape[1] == 1:
            shift = m - b
        elif b is not None:
            s = s + b
        if c == i:
            s = _causal(s, row0 - i * TQ)
        p = jnp.exp2(s - shift).astype(BF16)
        t = jnp.dot(p, vaug_ref[c * TQ:(c + 1) * TQ, :], preferred_element_type=F32)
        acc = t if acc is None else acc + t
    return acc


def _normalised(o_aug):
    return o_aug[:, :LANES] / o_aug[:, LANES:]


def _attend_pairs(prepare, unit_terms, k_refs, vaug_ref, s_ref, p_ref, finish):
    overflowed = jnp.zeros((FAST_ROWS, 2 * LANES), F32)
    prepare(0)
    for i in range(NQ):
        if i + 1 < NQ:
            prepare(i + 1)
        for r0 in range(i * TQ, (i + 1) * TQ, FAST_ROWS):
            accs = [_fast_unit(*unit_terms(half, r0, FAST_ROWS), k_refs[half], vaug_ref, i, r0)
                    for half in (0, 1)]
            finite = jnp.abs(accs[0] + accs[1]) < F32_HUGE
            overflowed = overflowed + jnp.where(finite, 0.0, 1.0)
            finish(r0, _normalised(accs[0]), _normalised(accs[1]))

    @pl.when(jnp.sum(overflowed) > 0.0)
    def _():
        _attend_pairs_two_pass(unit_terms, k_refs, vaug_ref, s_ref, p_ref, finish)


def _scores(q, bias, k_ref, s_ref, i):
    for c in range(i + 1):
        s = _dot_nt(q, k_ref[c * TQ:(c + 1) * TQ, :])
        b = bias(c)
        if b is not None:
            s = s + b
        if c == i:
            s = _causal(s, 0)
        s_ref[:, c * TQ:(c + 1) * TQ] = s


def _softmax_pv(s_ref, p_ref, vaug_ref, i):
    w = (i + 1) * TQ
    n_groups = w // LANES
    mm = s_ref[:, 0:LANES]
    for g in range(1, n_groups):
        mm = jnp.maximum(mm, s_ref[:, g * LANES:(g + 1) * LANES])
    m = jnp.max(mm, axis=-1, keepdims=True)
    for g in range(n_groups):
        p = jnp.exp2(s_ref[:, g * LANES:(g + 1) * LANES] - m)
        p_ref[:, g * LANES:(g + 1) * LANES] = p.astype(BF16)
    return jnp.dot(p_ref[:, :w], vaug_ref[:w, :], preferred_element_type=F32)


def _attend_pairs_two_pass(unit_terms, k_refs, vaug_ref, s_ref, p_ref, finish):
    units = [(i, half) for i in range(NQ) for half in (0, 1)]

    def scores(n):
        i, half = units[n]
        q, _, bias = unit_terms(half, i * TQ, TQ)
        _scores(q, bias, k_refs[half], s_ref.at[half], i)

    scores(0)
    outs = [None, None]
    for n, (i, half) in enumerate(units):
        if n + 1 < len(units):
            scores(n + 1)
        outs[half] = _normalised(_softmax_pv(s_ref.at[half], p_ref.at[half], vaug_ref, i))
        if half == 1:
            finish(i * TQ, outs[0], outs[1])


def _fill_vaug(vaug_ref, v_ref, i):
    rows = slice(i * TQ, (i + 1) * TQ)
    vaug_ref[rows, :LANES] = v_ref[rows, :]
    vaug_ref[rows, LANES:] = jnp.ones((TQ, LANES), BF16)


def _head_masks(shape):
    lane = lax.broadcasted_iota(jnp.int32, shape, 1)
    return lane < HEAD_DIM, lane >= HEAD_DIM


def _masked_q(q_ref, half, row0, rows):
    q2 = q_ref[row0:row0 + rows, :]
    return jnp.where(_head_masks(q2.shape)[half], q2, jnp.zeros_like(q2))


def _attn_call(kernel, extra_inputs, extra_specs, qkv, cols, n_pairs, scratch, name):
    cq, ck, cv = cols
    blk = (None, SEQ, LANES)
    return pl.pallas_call(
        kernel,
        grid=(BATCH, n_pairs),
        in_specs=list(extra_specs) + [
            pl.BlockSpec(blk, lambda b, p: (b, 0, cq + p)),
            pl.BlockSpec(blk, lambda b, p: (b, 0, ck + p)),
            pl.BlockSpec(blk, lambda b, p: (b, 0, cv + p))],
        out_specs=pl.BlockSpec(blk, lambda b, p: (b, 0, p)),
        out_shape=jax.ShapeDtypeStruct((BATCH, SEQ, n_pairs * LANES), BF16),
        scratch_shapes=[pltpu.VMEM((2, TQ, SEQ), F32), pltpu.VMEM((2, TQ, SEQ), BF16),
                        pltpu.VMEM((SEQ, 2 * LANES), BF16)] + list(scratch),
        compiler_params=_params(),
        name=name,
    )(*extra_inputs, qkv, qkv, qkv)


def _write_heads(o_ref, row0, o1, o2):
    o_ref[row0:row0 + o1.shape[0], :] = jnp.where(_head_masks(o1.shape)[0], o1, o2).astype(BF16)


def _diff_kernel(lam_ref, gsub_ref, q_ref, k_ref, v_ref, o_ref, s_ref, p_ref, vaug_ref, *, lam_init):
    lp = lam_ref[...]
    lam = (jnp.exp(jnp.sum(lp[0:1] * lp[1:2], axis=-1, keepdims=True))
           - jnp.exp(jnp.sum(lp[2:3] * lp[3:4], axis=-1, keepdims=True)) + lam_init)

    def prepare(i):
        _fill_vaug(vaug_ref, v_ref, i)

    def unit_terms(half, row0, rows):
        return _masked_q(q_ref, half, row0, rows), None, lambda c: None

    def finish(row0, o1, o2):
        o = o1 - lam * o2
        ms = jnp.mean(o * o, axis=-1, keepdims=True)
        o = (o * lax.rsqrt(ms + RMS_EPS) * gsub_ref[...]) * (1.0 - lam_init)
        o_ref[row0:row0 + o.shape[0], :] = o.astype(BF16)

    _attend_pairs(prepare, unit_terms, (k_ref, k_ref), vaug_ref, s_ref, p_ref, finish)


def _diff_attention(qkv, lam_params, g_subln, lam_init):
    return _attn_call(
        functools.partial(_diff_kernel, lam_init=lam_init),
        (lam_params, g_subln),
        (pl.BlockSpec((4, HEAD_DIM), lambda b, p: (0, 0)),
         pl.BlockSpec((1, 2 * HEAD_DIM), lambda b, p: (0, 0))),
        qkv, (COL_QA, COL_KA, COL_VA), N_HEADS, (), "diff_attention")


def _fox_kernel(logf_ref, q_ref, k_ref, v_ref, o_ref, s_ref, p_ref, vaug_ref, cum_ref, nckt_ref):
    pair = pl.program_id(1)
    tri = (lax.broadcasted_iota(jnp.int32, (TQ, TQ), 0)
           >= lax.broadcasted_iota(jnp.int32, (TQ, TQ), 1)).astype(BF16)
    lane = lax.broadcasted_iota(jnp.int32, (1, LANES), 1)
    carry = [jnp.zeros((1, LANES), F32)]

    def prepare(i):
        rows = slice(i * TQ, (i + 1) * TQ)
        _fill_vaug(vaug_ref, v_ref, i)
        lf = logf_ref[rows, :]
        hi = lf.astype(BF16)
        lo = (lf - hi.astype(F32)).astype(BF16)
        r = jnp.dot(tri, jnp.concatenate([hi, lo], axis=1), preferred_element_type=F32)
        cum = (r[:, :LANES] + r[:, LANES:]) + carry[0]
        carry[0] = cum[TQ - 1:TQ, :]
        cum2 = cum * LOG2E
        cum_ref[rows, :] = cum2
        nckt_ref[:, rows] = jnp.transpose(-cum2)[:N_HEADS * 2, :]

    def unit_terms(half, row0, rows):
        head = 2 * pair + half
        own = jnp.where(lane == head, cum_ref[row0:row0 + rows, :], 0.0)
        diag_bias = -jnp.sum(own, axis=-1, keepdims=True)
        return (_masked_q(q_ref, half, row0, rows), diag_bias,
                lambda c: nckt_ref[pl.ds(head, 1), c * TQ:(c + 1) * TQ])

    _attend_pairs(prepare, unit_terms, (k_ref, k_ref), vaug_ref, s_ref, p_ref,
                  functools.partial(_write_heads, o_ref))


def _fox_attention(qkv, logf):
    return _attn_call(
        _fox_kernel, (logf,),
        (pl.BlockSpec((None, SEQ, LANES), lambda b, p: (b, 0, 0)),),
        qkv, (COL_QB, COL_KB, COL_VB), N_HEADS // 2,
        (pltpu.VMEM((SEQ, LANES), F32), pltpu.VMEM((2 * N_HEADS, SEQ), F32)),
        "fox_attention")


def _moba_kernel(q_ref, k_ref, v_ref, o_ref, s_ref, p_ref, vaug_ref, qaug_ref, kaug_ref):
    for i in range(NQ):
        _fill_vaug(vaug_ref, v_ref, i)
    k2 = k_ref[...]
    kmean = jnp.sum(k2.astype(F32).reshape(N_KV_BLOCKS, MOBA_BLOCK, LANES), axis=1) * (1.0 / MOBA_BLOCK)
    km_hi = kmean.astype(BF16)
    km_lo = (kmean - km_hi.astype(F32)).astype(BF16)
    kmx = jnp.concatenate([km_hi, km_lo], axis=1)

    n_easy = (MOBA_TOPK + 1) * MOBA_BLOCK
    n_hard = SEQ - n_easy
    lane = lax.broadcasted_iota(jnp.int32, (SEQ, LANES), 1)
    block = lax.broadcasted_iota(jnp.int32, (SEQ, LANES), 0) // MOBA_BLOCK
    lane_h = lax.broadcasted_iota(jnp.int32, (n_hard, LANES), 1)
    block_h = lax.broadcasted_iota(jnp.int32, (n_hard, LANES), 0) // MOBA_BLOCK + (MOBA_TOPK + 1)
    lane_f = lane_h.astype(F32)
    for half in (0, 1):
        base = HEAD_DIM * (1 - half)
        own = _head_masks((SEQ, LANES))[half]
        kaug_ref[half] = jnp.where(own, k2, jnp.where(lane - base == block, 1.0, 0.0).astype(BF16))
        qaug_ref[half, :n_easy, :] = _masked_q(q_ref, half, 0, n_easy)
        pad = [jnp.zeros((n, 2 * LANES), BF16) for n in (base, LANES - base - N_KV_BLOCKS)]
        kmx_h = jnp.concatenate([t for t in (pad[0], kmx, pad[1]) if t.shape[0]], axis=0)
        qh = _masked_q(q_ref, half, n_easy, n_hard)
        gate = _dot_nt(jnp.concatenate([qh, qh], axis=1), kmx_h)
        e = lane_h - base
        past = jnp.logical_and(e >= 0, e < block_h)
        gate = jnp.where(past, gate, NEG_INF)
        bias = jnp.where(past, NEG_INF, 0.0)
        for _ in range(MOBA_TOPK):
            top = jnp.max(gate, axis=-1, keepdims=True)
            idx = jnp.min(jnp.where(gate == top, lane_f, float(LANES)), axis=-1, keepdims=True)
            hit = lane_f == idx
            bias = jnp.where(hit, 0.0, bias)
            gate = jnp.where(hit, NEG_INF, gate)
        own_h = _head_masks((n_hard, LANES))[half]
        qaug_ref[half, n_easy:, :] = jnp.where(own_h, q_ref[n_easy:, :], bias.astype(BF16))

    def unit_terms(half, row0, rows):
        return qaug_ref[half, row0:row0 + rows, :], None, lambda c: None

    _attend_pairs(lambda i: None, unit_terms, (kaug_ref.at[0], kaug_ref.at[1]), vaug_ref, s_ref, p_ref,
                  functools.partial(_write_heads, o_ref))


def _moba_attention(qkv):
    return _attn_call(
        _moba_kernel, (), (), qkv, (COL_QC, COL_KC, COL_VC), N_HEADS // 2,
        (pltpu.VMEM((2, SEQ, LANES), BF16), pltpu.VMEM((2, SEQ, LANES), BF16)),
        "moba_attention")


def _post_norm_residual(x, y, g, gate):
    ms = jnp.mean(y * y, axis=-1, keepdims=True)
    return x + gate * (y * lax.rsqrt(ms + RMS_EPS) * g)


def _merge_kernel(x_ref, mod_ref, g_ref, oa_ref, ob_ref, oc_ref, gate_ref,
                  wa_ref, wb_ref, wc_ref, wo_ref, o_ref):
    merged = None
    for j, (o_br, w_br) in enumerate(((oa_ref, wa_ref), (ob_ref, wb_ref), (oc_ref, wc_ref))):
        y = jnp.dot(o_br[...], w_br[...], preferred_element_type=F32)
        t = gate_ref[:, j * D_MODEL:(j + 1) * D_MODEL].astype(F32) * y
        merged = t if merged is None else merged + t
    y = jnp.dot(merged.astype(BF16), wo_ref[...], preferred_element_type=F32)
    o_ref[...] = _post_norm_residual(x_ref[...], y, g_ref[...], mod_ref[0, 2:3, :])


def _merge(x, mod, g_post, oa, ob, oc, gates, wa, wb, wc, wo):
    tok = lambda i: (i, 0)
    return pl.pallas_call(
        _merge_kernel,
        grid=(TOKENS // TM,),
        in_specs=[pl.BlockSpec((TM, D_MODEL), tok),
                  pl.BlockSpec((1, 6, D_MODEL), lambda i: (i // TILES_PER_BATCH, 0, 0)),
                  _resident((1, D_MODEL)),
                  pl.BlockSpec((TM, oa.shape[1]), tok),
                  pl.BlockSpec((TM, ob.shape[1]), tok),
                  pl.BlockSpec((TM, oc.shape[1]), tok),
                  pl.BlockSpec((TM, GATE_WIDTH), tok),
                  _resident(wa.shape), _resident(wb.shape), _resident(wc.shape),
                  _resident(wo.shape)],
        out_specs=pl.BlockSpec((TM, D_MODEL), tok),
        out_shape=jax.ShapeDtypeStruct((TOKENS, D_MODEL), F32),
        compiler_params=_params(),
        name="merge_out",
    )(x, mod, g_post, oa, ob, oc, gates, wa, wb, wc, wo)


FFN_CHUNKS = ((0, 1536), (1536, D_FF))


def _ffn_kernel(x_ref, mod_ref, gpre_ref, gpost_ref, wgu_ref, wd_ref, o_ref):
    for r0 in range(0, TM, SUB):
        x = x_ref[r0:r0 + SUB, :]
        hb = _modulated_norm(x, gpre_ref[...], mod_ref[0, 3:4, :], mod_ref[0, 4:5, :]).astype(BF16)
        y = None
        for c0, c1 in FFN_CHUNKS:
            a = jnp.dot(hb, wgu_ref[:, c0:c1], preferred_element_type=F32)
            b = jnp.dot(hb, wgu_ref[:, D_FF + c0:D_FF + c1], preferred_element_type=F32)
            act = (a * jax.nn.sigmoid(a) * b).astype(BF16)
            t = jnp.dot(act, wd_ref[c0:c1, :], preferred_element_type=F32)
            y = t if y is None else y + t
        o_ref[r0:r0 + SUB, :] = _post_norm_residual(x, y, gpost_ref[...], mod_ref[0, 5:6, :])


def _ffn(x, mod, g_pre, g_post, wgu, wd):
    tok = lambda i: (i, 0)
    return pl.pallas_call(
        _ffn_kernel,
        grid=(TOKENS // TM,),
        in_specs=[pl.BlockSpec((TM, D_MODEL), tok),
                  pl.BlockSpec((1, 6, D_MODEL), lambda i: (i // TILES_PER_BATCH, 0, 0)),
                  _resident((1, D_MODEL)), _resident((1, D_MODEL)),
                  _resident(wgu.shape), _resident(wd.shape)],
        out_specs=pl.BlockSpec((TM, D_MODEL), tok),
        out_shape=jax.ShapeDtypeStruct((TOKENS, D_MODEL), F32),
        compiler_params=_params(),
        name="swiglu",
    )(x, mod, g_pre, g_post, wgu, wd)


def kernel(x, c, positions, w_ada, b_ada, g_pre_mix, g_post_mix, w_in, b_fgt, lam_q1, lam_k1,
           lam_q2, lam_k2, g_subln, w_br_a, w_br_b, w_br_c, w_out, g_pre_ffn, g_post_ffn,
           w_gate_up, w_down):
    cos, sin = _rope_tables(positions)
    mod = _ada_mod(c, w_ada, b_ada).reshape(DEPTH, BATCH, 6, D_MODEL)

    scale = HEAD_DIM ** -0.5 * LOG2E
    col = jnp.arange(QKV_WIDTH + GATE_WIDTH)
    is_q = (col < 512) | ((col >= 1536) & (col < 1792)) | ((col >= 2304) & (col < 2560))
    col_scale = jnp.where(is_q, scale, 1.0).astype(F32)

    xt = x.reshape(TOKENS, D_MODEL)
    for l in range(DEPTH):
        lam_init = 0.8 - 0.6 * math.exp(-0.3 * l)
        n_main = QKV_WIDTH + GATE_WIDTH
        w_main = (w_in[l, :, :n_main] * col_scale).astype(BF16)
        w_fgt = jnp.pad(w_in[l, :, n_main:], ((0, 0), (0, LANES - N_HEADS))).astype(BF16)
        bf = jnp.pad(b_fgt[l], (0, LANES - N_HEADS))[None, :]
        qkv, gates, logf = _inproj(xt, mod[l], g_pre_mix[l][None, :], w_main, w_fgt, bf, cos, sin)

        qkv3 = qkv.reshape(BATCH, SEQ, QKV_WIDTH)
        lam_params = jnp.stack([lam_q1[l], lam_k1[l], lam_q2[l], lam_k2[l]])
        oa = _diff_attention(qkv3, lam_params, g_subln[l][None, :], lam_init)
        ob = _fox_attention(qkv3, logf.reshape(BATCH, SEQ, LANES))
        oc = _moba_attention(qkv3)

        xt = _merge(xt, mod[l], g_post_mix[l][None, :],
                    oa.reshape(TOKENS, -1), ob.reshape(TOKENS, -1), oc.reshape(TOKENS, -1), gates,
                    w_br_a[l].astype(BF16), w_br_b[l].astype(BF16), w_br_c[l].astype(BF16),
                    w_out[l].astype(BF16))
        xt = _ffn(xt, mod[l], g_pre_ffn[l][None, :], g_post_ffn[l][None, :],
                  w_gate_up[l].astype(BF16), w_down[l].astype(BF16))
    return xt.reshape(BATCH, SEQ, D_MODEL)
```

```python
import functools
import math

import jax
import jax.numpy as jnp
from jax import lax
from jax.experimental import pallas as pl
from jax.experimental.pallas import tpu as pltpu

D_MODEL = 1024
BATCH = 16
SEQ = 2048
DEPTH = 2
HEAD_DIM = 64
N_HEADS = 4
QKV_WIDTH = 3072
GATE_WIDTH = 3 * D_MODEL
D_FF = 2816
ROPE_THETA = 10000.0
MOBA_BLOCK = 256
MOBA_TOPK = 3
RMS_EPS = 1e-6
NEG_INF = -1e30
LOG2E = math.log2(math.e)

LANES = 128
TOKENS = BATCH * SEQ
TM = 512
TILES_PER_BATCH = SEQ // TM
SUB = 256
TQ = 256
NQ = SEQ // TQ
N_KV_BLOCKS = SEQ // MOBA_BLOCK
VMEM_LIMIT = 56 * 1024 * 1024

COL_QA, COL_KA, COL_VA = 0, 4, 8
COL_QB, COL_KB, COL_VB = 12, 14, 16
COL_QC, COL_KC, COL_VC = 18, 20, 22
QKV_SECTIONS = (
    (0, 512, True), (512, 512, True), (1024, 512, False),
    (1536, 256, False), (1792, 256, False), (2048, 256, False),
    (2304, 256, True), (2560, 256, True), (2816, 256, False),
)

F32 = jnp.float32
BF16 = jnp.bfloat16


def _params(**kw):
    return pltpu.CompilerParams(vmem_limit_bytes=VMEM_LIMIT, **kw)


def _resident(shape):
    nd = len(shape)
    return pl.BlockSpec(shape, lambda *_: (0,) * nd, pipeline_mode=pl.Buffered(1))


def _resident_layer(shape, l):
    return pl.BlockSpec((None,) + tuple(shape[1:]), lambda *_: (l, 0, 0), pipeline_mode=pl.Buffered(1))


def _dot_nt(a, b):
    return lax.dot_general(a, b, (((1,), (1,)), ((), ())), preferred_element_type=F32)


def _split3(x):
    hi = x.astype(BF16)
    r1 = x - hi.astype(F32)
    mid = r1.astype(BF16)
    lo = (r1 - mid.astype(F32)).astype(BF16)
    return hi, mid, lo


def _rope_table_kernel(pos_ref, invf_ref, sign_ref, cos_ref, sin_ref):
    ang = pos_ref[...].astype(F32) * invf_ref[...]
    cos_ref[...] = jnp.cos(ang)
    sin_ref[...] = jnp.sin(ang) * sign_ref[...]


def _rope_tables(positions):
    rows = 2048
    inv_freq = 1.0 / (ROPE_THETA ** (jnp.arange(0, HEAD_DIM, 2, dtype=F32) / HEAD_DIM))
    invf = jnp.tile(inv_freq, LANES // (HEAD_DIM // 2))[None, :]
    half = (jnp.arange(LANES) % HEAD_DIM) < HEAD_DIM // 2
    sign = jnp.where(half, -1.0, 1.0).astype(F32)[None, :]
    pos = positions.reshape(TOKENS, 1)
    return pl.pallas_call(
        _rope_table_kernel,
        grid=(TOKENS // rows,),
        in_specs=[pl.BlockSpec((rows, 1), lambda i: (i, 0)),
                  pl.BlockSpec((1, LANES), lambda i: (0, 0)),
                  pl.BlockSpec((1, LANES), lambda i: (0, 0))],
        out_specs=[pl.BlockSpec((rows, LANES), lambda i: (i, 0))] * 2,
        out_shape=[jax.ShapeDtypeStruct((TOKENS, LANES), F32)] * 2,
        compiler_params=_params(),
        name="rope_tables",
    )(pos, invf, sign)


def _ada_kernel(c_ref, w_ref, b_ref, o_ref):
    c = c_ref[...]
    a = _split3(c * jax.nn.sigmoid(c))
    w = _split3(w_ref[0])
    acc = b_ref[0]
    for i, j in ((2, 0), (1, 1), (0, 2), (1, 0), (0, 1), (0, 0)):
        acc = acc + jnp.dot(a[i], w[j], preferred_element_type=F32)
    o_ref[0] = acc


def _ada_mod(c, w_ada, b_ada):
    tn = 1536
    n = 6 * D_MODEL
    return pl.pallas_call(
        _ada_kernel,
        grid=(DEPTH, n // tn),
        in_specs=[pl.BlockSpec((BATCH, D_MODEL), lambda l, j: (0, 0)),
                  pl.BlockSpec((1, D_MODEL, tn), lambda l, j: (l, 0, j)),
                  pl.BlockSpec((1, 1, tn), lambda l, j: (l, 0, j))],
        out_specs=pl.BlockSpec((1, BATCH, tn), lambda l, j: (l, 0, j)),
        out_shape=jax.ShapeDtypeStruct((DEPTH, BATCH, n), F32),
        compiler_params=_params(),
        name="ada_mod",
    )(c, w_ada, b_ada.reshape(DEPTH, 1, n))


D_IN = QKV_WIDTH + GATE_WIDTH + N_HEADS
W_ALL = QKV_WIDTH + GATE_WIDTH + LANES
W_BLOCK = 896


def _win_prep_kernel(w_ref, scale_ref, o_ref):
    col = pl.program_id(1) * W_BLOCK + lax.broadcasted_iota(jnp.int32, (1, W_BLOCK), 1)
    o_ref[0] = jnp.where(col < D_IN, w_ref[0] * scale_ref[...], 0.0).astype(BF16)


def _win_prep(w_in, col_scale):
    return pl.pallas_call(
        _win_prep_kernel,
        grid=(DEPTH, W_ALL // W_BLOCK),
        in_specs=[pl.BlockSpec((1, D_MODEL, W_BLOCK), lambda l, j: (l, 0, j)),
                  pl.BlockSpec((1, W_BLOCK), lambda l, j: (0, j))],
        out_specs=pl.BlockSpec((1, D_MODEL, W_BLOCK), lambda l, j: (l, 0, j)),
        out_shape=jax.ShapeDtypeStruct((DEPTH, D_MODEL, W_ALL), BF16),
        compiler_params=_params(),
        name="win_prep",
    )(w_in, col_scale)


def _modulated_norm(x, g, shift, scale):
    ms = jnp.mean(x * x, axis=-1, keepdims=True)
    return (x * lax.rsqrt(ms + RMS_EPS) * g) * (1.0 + scale) + shift


def _inproj_kernel(x_ref, mod_ref, g_ref, w_ref, bf_ref, cos_ref, sin_ref,
                   qkv_ref, gate_ref, logf_ref):
    lane = lax.broadcasted_iota(jnp.int32, (1, LANES), 1)
    first_half = (lane % HEAD_DIM) < HEAD_DIM // 2
    for r0 in range(0, TM, SUB):
        rows = slice(r0, r0 + SUB)
        h = _modulated_norm(x_ref[rows, :], g_ref[...], mod_ref[0, 0:1, :], mod_ref[0, 1:2, :])
        hb = h.astype(BF16)
        cos = cos_ref[rows, :]
        sin = sin_ref[rows, :]

        def rope(t):
            rot = jnp.where(first_half, pltpu.roll(t, LANES - HEAD_DIM // 2, 1),
                            pltpu.roll(t, HEAD_DIM // 2, 1))
            return t * cos + rot * sin

        for c0, width, roped in QKV_SECTIONS:
            acc = jnp.dot(hb, w_ref[:, c0:c0 + width], preferred_element_type=F32)
            if roped:
                for g in range(width // LANES):
                    qkv_ref[rows, c0 + g * LANES:c0 + (g + 1) * LANES] = rope(
                        acc[:, g * LANES:(g + 1) * LANES]).astype(BF16)
            else:
                qkv_ref[rows, c0:c0 + width] = acc.astype(BF16)

        for c0 in range(0, GATE_WIDTH, D_MODEL):
            acc = jnp.dot(hb, w_ref[:, QKV_WIDTH + c0:QKV_WIDTH + c0 + D_MODEL],
                          preferred_element_type=F32)
            gate_ref[rows, c0:c0 + D_MODEL] = jax.nn.sigmoid(acc).astype(BF16)

        z = jnp.dot(hb, w_ref[:, QKV_WIDTH + GATE_WIDTH:], preferred_element_type=F32) + bf_ref[...]
        logf_ref[rows, :] = jnp.minimum(z, 0.0) - jnp.log1p(jnp.exp(-jnp.abs(z)))


def _inproj(x, mod, g_pre, w_all, l, b_fgt, cos, sin):
    tok = lambda i: (i, 0)
    return pl.pallas_call(
        _inproj_kernel,
        grid=(TOKENS // TM,),
        in_specs=[pl.BlockSpec((TM, D_MODEL), tok),
                  pl.BlockSpec((1, 6, D_MODEL), lambda i: (i // TILES_PER_BATCH, 0, 0)),
                  _resident((1, D_MODEL)),
                  _resident_layer(w_all.shape, l),
                  _resident((1, LANES)),
                  pl.BlockSpec((TM, LANES), tok),
                  pl.BlockSpec((TM, LANES), tok)],
        out_specs=[pl.BlockSpec((TM, QKV_WIDTH), tok),
                   pl.BlockSpec((TM, GATE_WIDTH), tok),
                   pl.BlockSpec((TM, LANES), tok)],
        out_shape=[jax.ShapeDtypeStruct((TOKENS, QKV_WIDTH), BF16),
                   jax.ShapeDtypeStruct((TOKENS, GATE_WIDTH), BF16),
                   jax.ShapeDtypeStruct((TOKENS, LANES), F32)],
        compiler_params=_params(),
        name="inproj",
    )(x, mod, g_pre, w_all, b_fgt, cos, sin)


FAST_ROWS = 128
F32_HUGE = 3.0e38


def _causal(s, row_offset):
    row = lax.broadcasted_iota(jnp.int32, s.shape, 0) + row_offset
    col = lax.broadcasted_iota(jnp.int32, s.shape, 1)
    return jnp.where(col <= row, s, NEG_INF)


def _fast_unit(q, diag_bias, bias, k_ref, vaug_ref, i, row0):
    r = q.shape[0]
    m = jnp.sum(q.astype(F32) * k_ref[row0:row0 + r, :].astype(F32), axis=-1, keepdims=True)
    if diag_bias is not None:
        m = m + diag_bias
    acc = None
    for c in range(i + 1):
        s = _dot_nt(q, k_ref[c * TQ:(c + 1) * TQ, :])
        shift = m
        b = bias(c)
        if b is not None and b.shape[1] == 1:
            shift = m - b
        elif b is not None:
            s = s + b
        if c == i:
            s = _causal(s, row0 - i * TQ)
        p = jnp.exp2(s - shift).astype(BF16)
        t = jnp.dot(p, vaug_ref[c * TQ:(c + 1) * TQ, :], preferred_element_type=F32)
        acc = t if acc is None else acc + t
    return acc


def _normalised(o_aug):
    return o_aug[:, :LANES] / o_aug[:, LANES:]


def _attend_pairs(prepare, unit_terms, k_refs, vaug_ref, s_ref, p_ref, finish):
    overflowed = jnp.zeros((FAST_ROWS, 2 * LANES), F32)
    prepare(0)
    for i in range(NQ):
        if i + 1 < NQ:
            prepare(i + 1)
        for r0 in range(i * TQ, (i + 1) * TQ, FAST_ROWS):
            accs = [_fast_unit(*unit_terms(half, r0, FAST_ROWS), k_refs[half], vaug_ref, i, r0)
                    for half in (0, 1)]
            finite = jnp.abs(accs[0] + accs[1]) < F32_HUGE
            overflowed = overflowed + jnp.where(finite, 0.0, 1.0)
            finish(r0, _normalised(accs[0]), _normalised(accs[1]))

    @pl.when(jnp.sum(overflowed) > 0.0)
    def _():
        _attend_pairs_two_pass(unit_terms, k_refs, vaug_ref, s_ref, p_ref, finish)


def _scores(q, bias, k_ref, s_ref, i):
    for c in range(i + 1):
        s = _dot_nt(q, k_ref[c * TQ:(c + 1) * TQ, :])
        b = bias(c)
        if b is not None:
            s = s + b
        if c == i:
            s = _causal(s, 0)
        s_ref[:, c * TQ:(c + 1) * TQ] = s


def _softmax_pv(s_ref, p_ref, vaug_ref, i):
    w = (i + 1) * TQ
    n_groups = w // LANES
    mm = s_ref[:, 0:LANES]
    for g in range(1, n_groups):
        mm = jnp.maximum(mm, s_ref[:, g * LANES:(g + 1) * LANES])
    m = jnp.max(mm, axis=-1, keepdims=True)
    for g in range(n_groups):
        p = jnp.exp2(s_ref[:, g * LANES:(g + 1) * LANES] - m)
        p_ref[:, g * LANES:(g + 1) * LANES] = p.astype(BF16)
    return jnp.dot(p_ref[:, :w], vaug_ref[:w, :], preferred_element_type=F32)


def _attend_pairs_two_pass(unit_terms, k_refs, vaug_ref, s_ref, p_ref, finish):
    units = [(i, half) for i in range(NQ) for half in (0, 1)]

    def scores(n):
        i, half = units[n]
        q, _, bias = unit_terms(half, i * TQ, TQ)
        _scores(q, bias, k_refs[half], s_ref.at[half], i)

    scores(0)
    outs = [None, None]
    for n, (i, half) in enumerate(units):
        if n + 1 < len(units):
            scores(n + 1)
        outs[half] = _normalised(_softmax_pv(s_ref.at[half], p_ref.at[half], vaug_ref, i))
        if half == 1:
            finish(i * TQ, outs[0], outs[1])


def _fill_vaug(vaug_ref, v_ref, i):
    rows = slice(i * TQ, (i + 1) * TQ)
    vaug_ref[rows, :LANES] = v_ref[rows, :]
    vaug_ref[rows, LANES:] = jnp.ones((TQ, LANES), BF16)


def _head_masks(shape):
    lane = lax.broadcasted_iota(jnp.int32, shape, 1)
    return lane < HEAD_DIM, lane >= HEAD_DIM


def _masked_q(q_ref, half, row0, rows):
    q2 = q_ref[row0:row0 + rows, :]
    return jnp.where(_head_masks(q2.shape)[half], q2, jnp.zeros_like(q2))


def _attn_call(kernel, extra_inputs, extra_specs, qkv, cols, n_pairs, scratch, name):
    cq, ck, cv = cols
    blk = (None, SEQ, LANES)
    return pl.pallas_call(
        kernel,
        grid=(BATCH, n_pairs),
        in_specs=list(extra_specs) + [
            pl.BlockSpec(blk, lambda b, p: (b, 0, cq + p)),
            pl.BlockSpec(blk, lambda b, p: (b, 0, ck + p)),
            pl.BlockSpec(blk, lambda b, p: (b, 0, cv + p))],
        out_specs=pl.BlockSpec(blk, lambda b, p: (b, 0, p)),
        out_shape=jax.ShapeDtypeStruct((BATCH, SEQ, n_pairs * LANES), BF16),
        scratch_shapes=[pltpu.VMEM((2, TQ, SEQ), F32), pltpu.VMEM((2, TQ, SEQ), BF16),
                        pltpu.VMEM((SEQ, 2 * LANES), BF16)] + list(scratch),
        compiler_params=_params(),
        name=name,
    )(*extra_inputs, qkv, qkv, qkv)


def _write_heads(o_ref, row0, o1, o2):
    o_ref[row0:row0 + o1.shape[0], :] = jnp.where(_head_masks(o1.shape)[0], o1, o2).astype(BF16)


def _diff_kernel(lam_ref, gsub_ref, q_ref, k_ref, v_ref, o_ref, s_ref, p_ref, vaug_ref, *, lam_init):
    lp = lam_ref[...]
    lam = (jnp.exp(jnp.sum(lp[0:1] * lp[1:2], axis=-1, keepdims=True))
           - jnp.exp(jnp.sum(lp[2:3] * lp[3:4], axis=-1, keepdims=True)) + lam_init)

    def prepare(i):
        _fill_vaug(vaug_ref, v_ref, i)

    def unit_terms(half, row0, rows):
        return _masked_q(q_ref, half, row0, rows), None, lambda c: None

    def finish(row0, o1, o2):
        o = o1 - lam * o2
        ms = jnp.mean(o * o, axis=-1, keepdims=True)
        o = (o * lax.rsqrt(ms + RMS_EPS) * gsub_ref[...]) * (1.0 - lam_init)
        o_ref[row0:row0 + o.shape[0], :] = o.astype(BF16)

    _attend_pairs(prepare, unit_terms, (k_ref, k_ref), vaug_ref, s_ref, p_ref, finish)


def _diff_attention(qkv, lam_params, g_subln, lam_init):
    return _attn_call(
        functools.partial(_diff_kernel, lam_init=lam_init),
        (lam_params, g_subln),
        (pl.BlockSpec((4, HEAD_DIM), lambda b, p: (0, 0)),
         pl.BlockSpec((1, 2 * HEAD_DIM), lambda b, p: (0, 0))),
        qkv, (COL_QA, COL_KA, COL_VA), N_HEADS, (), "diff_attention")


def _fox_kernel(logf_ref, q_ref, k_ref, v_ref, o_ref, s_ref, p_ref, vaug_ref, cum_ref, nckt_ref):
    pair = pl.program_id(1)
    tri = (lax.broadcasted_iota(jnp.int32, (TQ, TQ), 0)
           >= lax.broadcasted_iota(jnp.int32, (TQ, TQ), 1)).astype(BF16)
    lane = lax.broadcasted_iota(jnp.int32, (1, LANES), 1)
    carry = [jnp.zeros((1, LANES), F32)]

    def prepare(i):
        rows = slice(i * TQ, (i + 1) * TQ)
        _fill_vaug(vaug_ref, v_ref, i)
        lf = logf_ref[rows, :]
        hi = lf.astype(BF16)
        lo = (lf - hi.astype(F32)).astype(BF16)
        r = jnp.dot(tri, jnp.concatenate([hi, lo], axis=1), preferred_element_type=F32)
        cum = (r[:, :LANES] + r[:, LANES:]) + carry[0]
        carry[0] = cum[TQ - 1:TQ, :]
        cum2 = cum * LOG2E
        cum_ref[rows, :] = cum2
        nckt_ref[:, rows] = jnp.transpose(-cum2)[:N_HEADS * 2, :]

    def unit_terms(half, row0, rows):
        head = 2 * pair + half
        own = jnp.where(lane == head, cum_ref[row0:row0 + rows, :], 0.0)
        diag_bias = -jnp.sum(own, axis=-1, keepdims=True)
        return (_masked_q(q_ref, half, row0, rows), diag_bias,
                lambda c: nckt_ref[pl.ds(head, 1), c * TQ:(c + 1) * TQ])

    _attend_pairs(prepare, unit_terms, (k_ref, k_ref), vaug_ref, s_ref, p_ref,
                  functools.partial(_write_heads, o_ref))


def _fox_attention(qkv, logf):
    return _attn_call(
        _fox_kernel, (logf,),
        (pl.BlockSpec((None, SEQ, LANES), lambda b, p: (b, 0, 0)),),
        qkv, (COL_QB, COL_KB, COL_VB), N_HEADS // 2,
        (pltpu.VMEM((SEQ, LANES), F32), pltpu.VMEM((2 * N_HEADS, SEQ), F32)),
        "fox_attention")


def _moba_kernel(q_ref, k_ref, v_ref, o_ref, s_ref, p_ref, vaug_ref, qhard_ref, kaug_ref):
    for i in range(NQ):
        _fill_vaug(vaug_ref, v_ref, i)
    k2 = k_ref[...]
    kmean = jnp.sum(k2.astype(F32).reshape(N_KV_BLOCKS, MOBA_BLOCK, LANES), axis=1) * (1.0 / MOBA_BLOCK)
    km_hi = kmean.astype(BF16)
    km_lo = (kmean - km_hi.astype(F32)).astype(BF16)
    kmx = jnp.concatenate([km_hi, km_lo], axis=1)

    n_easy = (MOBA_TOPK + 1) * MOBA_BLOCK
    n_hard = SEQ - n_easy
    lane = lax.broadcasted_iota(jnp.int32, (SEQ, LANES), 1)
    block = lax.broadcasted_iota(jnp.int32, (SEQ, LANES), 0) // MOBA_BLOCK
    lane_h = lax.broadcasted_iota(jnp.int32, (n_hard, LANES), 1)
    block_h = lax.broadcasted_iota(jnp.int32, (n_hard, LANES), 0) // MOBA_BLOCK + (MOBA_TOPK + 1)
    lane_f = lane_h.astype(F32)
    for half in (0, 1):
        base = HEAD_DIM * (1 - half)
        own = _head_masks((SEQ, LANES))[half]
        kaug_ref[half] = jnp.where(own, k2, jnp.where(lane - base == block, 1.0, 0.0).astype(BF16))
        pad = [jnp.zeros((n, 2 * LANES), BF16) for n in (base, LANES - base - N_KV_BLOCKS)]
        kmx_h = jnp.concatenate([t for t in (pad[0], kmx, pad[1]) if t.shape[0]], axis=0)
        qh = _masked_q(q_ref, half, n_easy, n_hard)
        gate = _dot_nt(jnp.concatenate([qh, qh], axis=1), kmx_h)
        e = lane_h - base
        past = jnp.logical_and(e >= 0, e < block_h)
        gate = jnp.where(past, gate, NEG_INF)
        bias = jnp.where(past, NEG_INF, 0.0)
        for _ in range(MOBA_TOPK):
            top = jnp.max(gate, axis=-1, keepdims=True)
            idx = jnp.min(jnp.where(gate == top, lane_f, float(LANES)), axis=-1, keepdims=True)
            hit = lane_f == idx
            bias = jnp.where(hit, 0.0, bias)
            gate = jnp.where(hit, NEG_INF, gate)
        own_h = _head_masks((n_hard, LANES))[half]
        qhard_ref[half] = jnp.where(own_h, q_ref[n_easy:, :], bias.astype(BF16))

    def unit_terms(half, row0, rows):
        if row0 < n_easy:
            return _masked_q(q_ref, half, row0, rows), None, lambda c: None
        return qhard_ref[half, row0 - n_easy:row0 - n_easy + rows, :], None, lambda c: None

    _attend_pairs(lambda i: None, unit_terms, (kaug_ref.at[0], kaug_ref.at[1]), vaug_ref, s_ref, p_ref,
                  functools.partial(_write_heads, o_ref))


def _moba_attention(qkv):
    return _attn_call(
        _moba_kernel, (), (), qkv, (COL_QC, COL_KC, COL_VC), N_HEADS // 2,
        (pltpu.VMEM((2, SEQ - (MOBA_TOPK + 1) * MOBA_BLOCK, LANES), BF16),
         pltpu.VMEM((2, SEQ, LANES), BF16)),
        "moba_attention")


def _post_norm_residual(x, y, g, gate):
    ms = jnp.mean(y * y, axis=-1, keepdims=True)
    return x + gate * (y * lax.rsqrt(ms + RMS_EPS) * g)


def _merge_kernel(x_ref, mod_ref, g_ref, oa_ref, ob_ref, oc_ref, gate_ref,
                  wa_ref, wb_ref, wc_ref, wo_ref, o_ref):
    merged = None
    for j, (o_br, w_br) in enumerate(((oa_ref, wa_ref), (ob_ref, wb_ref), (oc_ref, wc_ref))):
        y = jnp.dot(o_br[...], w_br[...], preferred_element_type=F32)
        t = gate_ref[:, j * D_MODEL:(j + 1) * D_MODEL].astype(F32) * y
        merged = t if merged is None else merged + t
    y = jnp.dot(merged.astype(BF16), wo_ref[...], preferred_element_type=F32)
    o_ref[...] = _post_norm_residual(x_ref[...], y, g_ref[...], mod_ref[0, 2:3, :])


def _merge(x, mod, g_post, oa, ob, oc, gates, wa, wb, wc, wo, l):
    tok = lambda i: (i, 0)
    return pl.pallas_call(
        _merge_kernel,
        grid=(TOKENS // TM,),
        in_specs=[pl.BlockSpec((TM, D_MODEL), tok),
                  pl.BlockSpec((1, 6, D_MODEL), lambda i: (i // TILES_PER_BATCH, 0, 0)),
                  _resident((1, D_MODEL)),
                  pl.BlockSpec((TM, oa.shape[1]), tok),
                  pl.BlockSpec((TM, ob.shape[1]), tok),
                  pl.BlockSpec((TM, oc.shape[1]), tok),
                  pl.BlockSpec((TM, GATE_WIDTH), tok),
                  _resident_layer(wa.shape, l), _resident_layer(wb.shape, l),
                  _resident_layer(wc.shape, l), _resident_layer(wo.shape, l)],
        out_specs=pl.BlockSpec((TM, D_MODEL), tok),
        out_shape=jax.ShapeDtypeStruct((TOKENS, D_MODEL), F32),
        compiler_params=_params(),
        name="merge_out",
    )(x, mod, g_post, oa, ob, oc, gates, wa, wb, wc, wo)


FFN_CHUNKS = ((0, 1536), (1536, D_FF))


def _ffn_kernel(x_ref, mod_ref, gpre_ref, gpost_ref, wgu_ref, wd_ref, o_ref):
    for r0 in range(0, TM, SUB):
        x = x_ref[r0:r0 + SUB, :]
        hb = _modulated_norm(x, gpre_ref[...], mod_ref[0, 3:4, :], mod_ref[0, 4:5, :]).astype(BF16)
        y = None
        for c0, c1 in FFN_CHUNKS:
            a = jnp.dot(hb, wgu_ref[:, c0:c1], preferred_element_type=F32)
            b = jnp.dot(hb, wgu_ref[:, D_FF + c0:D_FF + c1], preferred_element_type=F32)
            act = (a * jax.nn.sigmoid(a) * b).astype(BF16)
            t = jnp.dot(act, wd_ref[c0:c1, :], preferred_element_type=F32)
            y = t if y is None else y + t
        o_ref[r0:r0 + SUB, :] = _post_norm_residual(x, y, gpost_ref[...], mod_ref[0, 5:6, :])


def _ffn(x, mod, g_pre, g_post, wgu, wd, l):
    tok = lambda i: (i, 0)
    return pl.pallas_call(
        _ffn_kernel,
        grid=(TOKENS // TM,),
        in_specs=[pl.BlockSpec((TM, D_MODEL), tok),
                  pl.BlockSpec((1, 6, D_MODEL), lambda i: (i // TILES_PER_BATCH, 0, 0)),
                  _resident((1, D_MODEL)), _resident((1, D_MODEL)),
                  _resident_layer(wgu.shape, l), _resident_layer(wd.shape, l)],
        out_specs=pl.BlockSpec((TM, D_MODEL), tok),
        out_shape=jax.ShapeDtypeStruct((TOKENS, D_MODEL), F32),
        compiler_params=_params(),
        name="swiglu",
    )(x, mod, g_pre, g_post, wgu, wd)


def kernel(x, c, positions, w_ada, b_ada, g_pre_mix, g_post_mix, w_in, b_fgt, lam_q1, lam_k1,
           lam_q2, lam_k2, g_subln, w_br_a, w_br_b, w_br_c, w_out, g_pre_ffn, g_post_ffn,
           w_gate_up, w_down):
    cos, sin = _rope_tables(positions)
    mod = _ada_mod(c, w_ada, b_ada).reshape(DEPTH, BATCH, 6, D_MODEL)

    scale = HEAD_DIM ** -0.5 * LOG2E
    col = jnp.arange(W_ALL)
    is_q = (col < 512) | ((col >= 1536) & (col < 1792)) | ((col >= 2304) & (col < 2560))
    w_all = _win_prep(w_in, jnp.where(is_q, scale, 1.0).astype(F32)[None, :])
    wa, wb, wc, wo = (w.astype(BF16) for w in (w_br_a, w_br_b, w_br_c, w_out))
    wgu, wd = w_gate_up.astype(BF16), w_down.astype(BF16)

    xt = x.reshape(TOKENS, D_MODEL)
    for l in range(DEPTH):
        lam_init = 0.8 - 0.6 * math.exp(-0.3 * l)
        bf = jnp.pad(b_fgt[l], (0, LANES - N_HEADS))[None, :]
        qkv, gates, logf = _inproj(xt, mod[l], g_pre_mix[l][None, :], w_all, l, bf, cos, sin)

        qkv3 = qkv.reshape(BATCH, SEQ, QKV_WIDTH)
        lam_params = jnp.stack([lam_q1[l], lam_k1[l], lam_q2[l], lam_k2[l]])
        oa = _diff_attention(qkv3, lam_params, g_subln[l][None, :], lam_init)
        ob = _fox_attention(qkv3, logf.reshape(BATCH, SEQ, LANES))
        oc = _moba_attention(qkv3)

        xt = _merge(xt, mod[l], g_post_mix[l][None, :],
                    oa.reshape(TOKENS, -1), ob.reshape(TOKENS, -1), oc.reshape(TOKENS, -1), gates,
                    wa, wb, wc, wo, l)
        xt = _ffn(xt, mod[l], g_pre_ffn[l][None, :], g_post_ffn[l][None, :], wgu, wd, l)
    return xt.reshape(BATCH, SEQ, D_MODEL)
```

```python
import functools
import math

import jax
import jax.numpy as jnp
from jax import lax
from jax.experimental import pallas as pl
from jax.experimental.pallas import tpu as pltpu

D_MODEL = 1024
BATCH = 16
SEQ = 2048
DEPTH = 2
HEAD_DIM = 64
N_HEADS = 4
QKV_WIDTH = 3072
GATE_WIDTH = 3 * D_MODEL
D_FF = 2816
ROPE_THETA = 10000.0
MOBA_BLOCK = 256
MOBA_TOPK = 3
RMS_EPS = 1e-6
NEG_INF = -1e30
LOG2E = math.log2(math.e)

LANES = 128
TOKENS = BATCH * SEQ
TM = 512
TILES_PER_BATCH = SEQ // TM
SUB = 256
TQ = 256
NQ = SEQ // TQ
N_KV_BLOCKS = SEQ // MOBA_BLOCK
VMEM_LIMIT = 56 * 1024 * 1024

COL_QA, COL_KA, COL_VA = 0, 4, 8
COL_QB, COL_KB, COL_VB = 12, 14, 16
COL_QC, COL_KC, COL_VC = 18, 20, 22
QKV_SECTIONS = (
    (0, 512, True), (512, 512, True), (1024, 512, False),
    (1536, 256, False), (1792, 256, False), (2048, 256, False),
    (2304, 256, True), (2560, 256, True), (2816, 256, False),
)

F32 = jnp.float32
BF16 = jnp.bfloat16


def _params(**kw):
    return pltpu.CompilerParams(vmem_limit_bytes=VMEM_LIMIT, **kw)


def _resident(shape):
    nd = len(shape)
    return pl.BlockSpec(shape, lambda *_: (0,) * nd, pipeline_mode=pl.Buffered(1))


def _resident_layer(shape, l):
    return pl.BlockSpec((None,) + tuple(shape[1:]), lambda *_: (l, 0, 0), pipeline_mode=pl.Buffered(1))


def _dot_nt(a, b):
    return lax.dot_general(a, b, (((1,), (1,)), ((), ())), preferred_element_type=F32)


def _split3(x):
    hi = x.astype(BF16)
    r1 = x - hi.astype(F32)
    mid = r1.astype(BF16)
    lo = (r1 - mid.astype(F32)).astype(BF16)
    return hi, mid, lo


def _rope_table_kernel(pos_ref, invf_ref, sign_ref, cos_ref, sin_ref):
    ang = pos_ref[...].astype(F32) * invf_ref[...]
    cos_ref[...] = jnp.cos(ang)
    sin_ref[...] = jnp.sin(ang) * sign_ref[...]


def _rope_tables(positions):
    inv_freq = 1.0 / (ROPE_THETA ** (jnp.arange(0, HEAD_DIM, 2, dtype=F32) / HEAD_DIM))
    invf = jnp.tile(inv_freq, LANES // (HEAD_DIM // 2))[None, :]
    half = (jnp.arange(LANES) % HEAD_DIM) < HEAD_DIM // 2
    sign = jnp.where(half, -1.0, 1.0).astype(F32)[None, :]

    def tables(pos):
        n = pos.shape[0]
        return pl.pallas_call(
            _rope_table_kernel,
            grid=(n // SEQ,),
            in_specs=[pl.BlockSpec((SEQ, 1), lambda i: (i, 0)),
                      pl.BlockSpec((1, LANES), lambda i: (0, 0)),
                      pl.BlockSpec((1, LANES), lambda i: (0, 0))],
            out_specs=[pl.BlockSpec((SEQ, LANES), lambda i: (i, 0))] * 2,
            out_shape=[jax.ShapeDtypeStruct((n, LANES), F32)] * 2,
            compiler_params=_params(),
            name="rope_tables",
        )(pos.reshape(n, 1), invf, sign)

    def shared(pos):
        return tuple(jnp.tile(t, (BATCH, 1)) for t in tables(pos[0]))

    def per_row(pos):
        return tuple(tables(pos.reshape(TOKENS)))

    return lax.cond(jnp.all(positions == positions[:1]), shared, per_row, positions)


def _ada_kernel(c_ref, w_ref, b_ref, o_ref):
    c = c_ref[...]
    a = _split3(c * jax.nn.sigmoid(c))
    w = _split3(w_ref[0])
    acc = b_ref[0]
    for i, j in ((2, 0), (1, 1), (0, 2), (1, 0), (0, 1), (0, 0)):
        acc = acc + jnp.dot(a[i], w[j], preferred_element_type=F32)
    o_ref[0] = acc


def _ada_mod(c, w_ada, b_ada):
    tn = 1536
    n = 6 * D_MODEL
    return pl.pallas_call(
        _ada_kernel,
        grid=(DEPTH, n // tn),
        in_specs=[pl.BlockSpec((BATCH, D_MODEL), lambda l, j: (0, 0)),
                  pl.BlockSpec((1, D_MODEL, tn), lambda l, j: (l, 0, j)),
                  pl.BlockSpec((1, 1, tn), lambda l, j: (l, 0, j))],
        out_specs=pl.BlockSpec((1, BATCH, tn), lambda l, j: (l, 0, j)),
        out_shape=jax.ShapeDtypeStruct((DEPTH, BATCH, n), F32),
        compiler_params=_params(),
        name="ada_mod",
    )(c, w_ada, b_ada.reshape(DEPTH, 1, n))


D_IN = QKV_WIDTH + GATE_WIDTH + N_HEADS
W_ALL = QKV_WIDTH + GATE_WIDTH + LANES
W_BLOCK = 896


def _win_prep_kernel(w_ref, scale_ref, o_ref):
    col = pl.program_id(1) * W_BLOCK + lax.broadcasted_iota(jnp.int32, (1, W_BLOCK), 1)
    o_ref[0] = jnp.where(col < D_IN, w_ref[0] * scale_ref[...], 0.0).astype(BF16)


def _win_prep(w_in, col_scale):
    return pl.pallas_call(
        _win_prep_kernel,
        grid=(DEPTH, W_ALL // W_BLOCK),
        in_specs=[pl.BlockSpec((1, D_MODEL, W_BLOCK), lambda l, j: (l, 0, j)),
                  pl.BlockSpec((1, W_BLOCK), lambda l, j: (0, j))],
        out_specs=pl.BlockSpec((1, D_MODEL, W_BLOCK), lambda l, j: (l, 0, j)),
        out_shape=jax.ShapeDtypeStruct((DEPTH, D_MODEL, W_ALL), BF16),
        compiler_params=_params(),
        name="win_prep",
    )(w_in, col_scale)


def _modulated_norm(x, g, shift, scale):
    ms = jnp.mean(x * x, axis=-1, keepdims=True)
    return (x * lax.rsqrt(ms + RMS_EPS) * g) * (1.0 + scale) + shift


def _inproj_kernel(x_ref, mod_ref, g_ref, w_ref, bf_ref, cos_ref, sin_ref,
                   qkv_ref, gate_ref, logf_ref):
    lane = lax.broadcasted_iota(jnp.int32, (1, LANES), 1)
    first_half = (lane % HEAD_DIM) < HEAD_DIM // 2
    for r0 in range(0, TM, SUB):
        rows = slice(r0, r0 + SUB)
        h = _modulated_norm(x_ref[rows, :], g_ref[...], mod_ref[0, 0:1, :], mod_ref[0, 1:2, :])
        hb = h.astype(BF16)
        cos = cos_ref[rows, :]
        sin = sin_ref[rows, :]

        def rope(t):
            rot = jnp.where(first_half, pltpu.roll(t, LANES - HEAD_DIM // 2, 1),
                            pltpu.roll(t, HEAD_DIM // 2, 1))
            return t * cos + rot * sin

        for c0, width, roped in QKV_SECTIONS:
            acc = jnp.dot(hb, w_ref[:, c0:c0 + width], preferred_element_type=F32)
            if roped:
                for g in range(width // LANES):
                    qkv_ref[rows, c0 + g * LANES:c0 + (g + 1) * LANES] = rope(
                        acc[:, g * LANES:(g + 1) * LANES]).astype(BF16)
            else:
                qkv_ref[rows, c0:c0 + width] = acc.astype(BF16)

        for c0 in range(0, GATE_WIDTH, D_MODEL):
            acc = jnp.dot(hb, w_ref[:, QKV_WIDTH + c0:QKV_WIDTH + c0 + D_MODEL],
                          preferred_element_type=F32)
            gate_ref[rows, c0:c0 + D_MODEL] = jax.nn.sigmoid(acc).astype(BF16)

        z = jnp.dot(hb, w_ref[:, QKV_WIDTH + GATE_WIDTH:], preferred_element_type=F32) + bf_ref[...]
        logf_ref[rows, :] = jnp.minimum(z, 0.0) - jnp.log1p(jnp.exp(-jnp.abs(z)))


def _inproj(x, mod, g_pre, w_all, l, b_fgt, cos, sin):
    tok = lambda i: (i, 0)
    return pl.pallas_call(
        _inproj_kernel,
        grid=(TOKENS // TM,),
        in_specs=[pl.BlockSpec((TM, D_MODEL), tok),
                  pl.BlockSpec((1, 6, D_MODEL), lambda i: (i // TILES_PER_BATCH, 0, 0)),
                  _resident((1, D_MODEL)),
                  _resident_layer(w_all.shape, l),
                  _resident((1, LANES)),
                  pl.BlockSpec((TM, LANES), tok),
                  pl.BlockSpec((TM, LANES), tok)],
        out_specs=[pl.BlockSpec((TM, QKV_WIDTH), tok),
                   pl.BlockSpec((TM, GATE_WIDTH), tok),
                   pl.BlockSpec((TM, LANES), tok)],
        out_shape=[jax.ShapeDtypeStruct((TOKENS, QKV_WIDTH), BF16),
                   jax.ShapeDtypeStruct((TOKENS, GATE_WIDTH), BF16),
                   jax.ShapeDtypeStruct((TOKENS, LANES), F32)],
        compiler_params=_params(),
        name="inproj",
    )(x, mod, g_pre, w_all, b_fgt, cos, sin)


FAST_ROWS = 128
F32_HUGE = 3.0e38


def _causal(s, row_offset):
    row = lax.broadcasted_iota(jnp.int32, s.shape, 0) + row_offset
    col = lax.broadcasted_iota(jnp.int32, s.shape, 1)
    return jnp.where(col <= row, s, NEG_INF)


def _fast_unit(q, diag_bias, bias, k_ref, vaug_ref, i, row0):
    r = q.shape[0]
    m = jnp.sum(q.astype(F32) * k_ref[row0:row0 + r, :].astype(F32), axis=-1, keepdims=True)
    if diag_bias is not None:
        m = m + diag_bias
    acc = None
    for c in range(i + 1):
        s = _dot_nt(q, k_ref[c * TQ:(c + 1) * TQ, :])
        shift = m
        b = bias(c)
        if b is not None and b.shape[1] == 1:
            shift = m - b
        elif b is not None:
            s = s + b
        if c == i:
            s = _causal(s, row0 - i * TQ)
        p = jnp.exp2(s - shift).astype(BF16)
        t = jnp.dot(p, vaug_ref[c * TQ:(c + 1) * TQ, :], preferred_element_type=F32)
        acc = t if acc is None else acc + t
    return acc


def _normalised(o_aug):
    return o_aug[:, :LANES] / o_aug[:, LANES:]


def _attend_pairs(prepare, unit_terms, k_refs, vaug_ref, s_ref, p_ref, finish):
    overflowed = jnp.zeros((FAST_ROWS, 2 * LANES), F32)
    prepare(0)
    for i in range(NQ):
        if i + 1 < NQ:
            prepare(i + 1)
        for r0 in range(i * TQ, (i + 1) * TQ, FAST_ROWS):
            accs = [_fast_unit(*unit_terms(half, r0, FAST_ROWS), k_refs[half], vaug_ref, i, r0)
                    for half in (0, 1)]
            finite = jnp.abs(accs[0] + accs[1]) < F32_HUGE
            overflowed = overflowed + jnp.where(finite, 0.0, 1.0)
            finish(r0, _normalised(accs[0]), _normalised(accs[1]))

    @pl.when(jnp.sum(overflowed) > 0.0)
    def _():
        _attend_pairs_two_pass(unit_terms, k_refs, vaug_ref, s_ref, p_ref, finish)


def _scores(q, bias, k_ref, s_ref, i):
    for c in range(i + 1):
        s = _dot_nt(q, k_ref[c * TQ:(c + 1) * TQ, :])
        b = bias(c)
        if b is not None:
            s = s + b
        if c == i:
            s = _causal(s, 0)
        s_ref[:, c * TQ:(c + 1) * TQ] = s


def _softmax_pv(s_ref, p_ref, vaug_ref, i):
    w = (i + 1) * TQ
    n_groups = w // LANES
    mm = s_ref[:, 0:LANES]
    for g in range(1, n_groups):
        mm = jnp.maximum(mm, s_ref[:, g * LANES:(g + 1) * LANES])
    m = jnp.max(mm, axis=-1, keepdims=True)
    for g in range(n_groups):
        p = jnp.exp2(s_ref[:, g * LANES:(g + 1) * LANES] - m)
        p_ref[:, g * LANES:(g + 1) * LANES] = p.astype(BF16)
    return jnp.dot(p_ref[:, :w], vaug_ref[:w, :], preferred_element_type=F32)


def _attend_pairs_two_pass(unit_terms, k_refs, vaug_ref, s_ref, p_ref, finish):
    units = [(i, half) for i in range(NQ) for half in (0, 1)]

    def scores(n):
        i, half = units[n]
        q, _, bias = unit_terms(half, i * TQ, TQ)
        _scores(q, bias, k_refs[half], s_ref.at[half], i)

    scores(0)
    outs = [None, None]
    for n, (i, half) in enumerate(units):
        if n + 1 < len(units):
            scores(n + 1)
        outs[half] = _normalised(_softmax_pv(s_ref.at[half], p_ref.at[half], vaug_ref, i))
        if half == 1:
            finish(i * TQ, outs[0], outs[1])


def _fill_vaug(vaug_ref, v_ref, i):
    rows = slice(i * TQ, (i + 1) * TQ)
    vaug_ref[rows, :LANES] = v_ref[rows, :]
    vaug_ref[rows, LANES:] = jnp.ones((TQ, LANES), BF16)


def _head_masks(shape):
    lane = lax.broadcasted_iota(jnp.int32, shape, 1)
    return lane < HEAD_DIM, lane >= HEAD_DIM


def _masked_q(q_ref, half, row0, rows):
    q2 = q_ref[row0:row0 + rows, :]
    return jnp.where(_head_masks(q2.shape)[half], q2, jnp.zeros_like(q2))


def _attn_call(kernel, extra_inputs, extra_specs, qkv, cols, n_pairs, scratch, name):
    cq, ck, cv = cols
    blk = (None, SEQ, LANES)
    return pl.pallas_call(
        kernel,
        grid=(BATCH, n_pairs),
        in_specs=list(extra_specs) + [
            pl.BlockSpec(blk, lambda b, p: (b, 0, cq + p)),
            pl.BlockSpec(blk, lambda b, p: (b, 0, ck + p)),
            pl.BlockSpec(blk, lambda b, p: (b, 0, cv + p))],
        out_specs=pl.BlockSpec(blk, lambda b, p: (b, 0, p)),
        out_shape=jax.ShapeDtypeStruct((BATCH, SEQ, n_pairs * LANES), BF16),
        scratch_shapes=[pltpu.VMEM((2, TQ, SEQ), F32), pltpu.VMEM((2, TQ, SEQ), BF16),
                        pltpu.VMEM((SEQ, 2 * LANES), BF16)] + list(scratch),
        compiler_params=_params(),
        name=name,
    )(*extra_inputs, qkv, qkv, qkv)


def _write_heads(o_ref, row0, o1, o2):
    o_ref[row0:row0 + o1.shape[0], :] = jnp.where(_head_masks(o1.shape)[0], o1, o2).astype(BF16)


def _diff_kernel(lam_ref, gsub_ref, q_ref, k_ref, v_ref, o_ref, s_ref, p_ref, vaug_ref, *, lam_init):
    lp = lam_ref[...]
    lam = (jnp.exp(jnp.sum(lp[0:1] * lp[1:2], axis=-1, keepdims=True))
           - jnp.exp(jnp.sum(lp[2:3] * lp[3:4], axis=-1, keepdims=True)) + lam_init)

    def prepare(i):
        _fill_vaug(vaug_ref, v_ref, i)

    def unit_terms(half, row0, rows):
        return _masked_q(q_ref, half, row0, rows), None, lambda c: None

    def finish(row0, o1, o2):
        o = o1 - lam * o2
        ms = jnp.mean(o * o, axis=-1, keepdims=True)
        o = (o * lax.rsqrt(ms + RMS_EPS) * gsub_ref[...]) * (1.0 - lam_init)
        o_ref[row0:row0 + o.shape[0], :] = o.astype(BF16)

    _attend_pairs(prepare, unit_terms, (k_ref, k_ref), vaug_ref, s_ref, p_ref, finish)


def _diff_attention(qkv, lam_params, g_subln, lam_init):
    return _attn_call(
        functools.partial(_diff_kernel, lam_init=lam_init),
        (lam_params, g_subln),
        (pl.BlockSpec((4, HEAD_DIM), lambda b, p: (0, 0)),
         pl.BlockSpec((1, 2 * HEAD_DIM), lambda b, p: (0, 0))),
        qkv, (COL_QA, COL_KA, COL_VA), N_HEADS, (), "diff_attention")


def _fox_kernel(logf_ref, q_ref, k_ref, v_ref, o_ref, s_ref, p_ref, vaug_ref, cum_ref, nckt_ref):
    pair = pl.program_id(1)
    tri = (lax.broadcasted_iota(jnp.int32, (TQ, TQ), 0)
           >= lax.broadcasted_iota(jnp.int32, (TQ, TQ), 1)).astype(BF16)
    lane = lax.broadcasted_iota(jnp.int32, (1, LANES), 1)
    carry = [jnp.zeros((1, LANES), F32)]

    def prepare(i):
        rows = slice(i * TQ, (i + 1) * TQ)
        _fill_vaug(vaug_ref, v_ref, i)
        lf = logf_ref[rows, :]
        hi = lf.astype(BF16)
        lo = (lf - hi.astype(F32)).astype(BF16)
        r = jnp.dot(tri, jnp.concatenate([hi, lo], axis=1), preferred_element_type=F32)
        cum = (r[:, :LANES] + r[:, LANES:]) + carry[0]
        carry[0] = cum[TQ - 1:TQ, :]
        cum2 = cum * LOG2E
        cum_ref[rows, :] = cum2
        nckt_ref[:, rows] = jnp.transpose(-cum2)[:N_HEADS * 2, :]

    def unit_terms(half, row0, rows):
        head = 2 * pair + half
        own = jnp.where(lane == head, cum_ref[row0:row0 + rows, :], 0.0)
        diag_bias = -jnp.sum(own, axis=-1, keepdims=True)
        return (_masked_q(q_ref, half, row0, rows), diag_bias,
                lambda c: nckt_ref[pl.ds(head, 1), c * TQ:(c + 1) * TQ])

    _attend_pairs(prepare, unit_terms, (k_ref, k_ref), vaug_ref, s_ref, p_ref,
                  functools.partial(_write_heads, o_ref))


def _fox_attention(qkv, logf):
    return _attn_call(
        _fox_kernel, (logf,),
        (pl.BlockSpec((None, SEQ, LANES), lambda b, p: (b, 0, 0)),),
        qkv, (COL_QB, COL_KB, COL_VB), N_HEADS // 2,
        (pltpu.VMEM((SEQ, LANES), F32), pltpu.VMEM((2 * N_HEADS, SEQ), F32)),
        "fox_attention")


def _moba_kernel(q_ref, k_ref, v_ref, o_ref, s_ref, p_ref, vaug_ref, qhard_ref, kaug_ref):
    for i in range(NQ):
        _fill_vaug(vaug_ref, v_ref, i)
    member = (lax.broadcasted_iota(jnp.int32, (N_KV_BLOCKS, SEQ), 1) // MOBA_BLOCK
              == lax.broadcasted_iota(jnp.int32, (N_KV_BLOCKS, SEQ), 0)).astype(BF16)
    kmean = jnp.dot(member, k_ref[...], preferred_element_type=F32) * (1.0 / MOBA_BLOCK)
    km_hi = kmean.astype(BF16)
    km_lo = (kmean - km_hi.astype(F32)).astype(BF16)
    kmx = jnp.concatenate([km_hi, km_lo], axis=1)

    n_easy = (MOBA_TOPK + 1) * MOBA_BLOCK
    n_hard = SEQ - n_easy
    lane_row = lax.broadcasted_iota(jnp.int32, (1, LANES), 1)
    blk = lax.broadcasted_iota(jnp.int32, (N_KV_BLOCKS, n_hard), 0)
    blk_f = blk.astype(F32)
    past = blk < lax.broadcasted_iota(jnp.int32, (N_KV_BLOCKS, n_hard), 1) // MOBA_BLOCK + (MOBA_TOPK + 1)
    for half in (0, 1):
        base = HEAD_DIM * (1 - half)
        own = _head_masks((TQ, LANES))[half]
        for n in range(N_KV_BLOCKS):
            rows = slice(n * MOBA_BLOCK, (n + 1) * MOBA_BLOCK)
            one_hot = jnp.where(lane_row == base + n, 1.0, 0.0).astype(BF16)
            kaug_ref[half, rows, :] = jnp.where(own, k_ref[rows, :], one_hot)
        qh = _masked_q(q_ref, half, n_easy, n_hard)
        gate = _dot_nt(kmx, jnp.concatenate([qh, qh], axis=1))
        gate = jnp.where(past, gate, NEG_INF)
        drop = jnp.where(past, -1.0, 0.0)
        for _ in range(MOBA_TOPK):
            top = jnp.max(gate, axis=0, keepdims=True)
            idx = jnp.min(jnp.where(gate == top, blk_f, float(N_KV_BLOCKS)), axis=0, keepdims=True)
            hit = blk_f == idx
            drop = jnp.where(hit, 0.0, drop)
            gate = jnp.where(hit, NEG_INF, gate)
        place = (lax.broadcasted_iota(jnp.int32, (N_KV_BLOCKS, LANES), 1)
                 == lax.broadcasted_iota(jnp.int32, (N_KV_BLOCKS, LANES), 0) + base).astype(BF16)
        dropped = lax.dot_general(drop.astype(BF16), place, (((0,), (0,)), ((), ())),
                                  preferred_element_type=F32)
        own_h = _head_masks((n_hard, LANES))[half]
        qhard_ref[half] = jnp.where(own_h, q_ref[n_easy:, :], (dropped * -NEG_INF).astype(BF16))

    def unit_terms(half, row0, rows):
        if row0 < n_easy:
            return _masked_q(q_ref, half, row0, rows), None, lambda c: None
        return qhard_ref[half, row0 - n_easy:row0 - n_easy + rows, :], None, lambda c: None

    _attend_pairs(lambda i: None, unit_terms, (kaug_ref.at[0], kaug_ref.at[1]), vaug_ref, s_ref, p_ref,
                  functools.partial(_write_heads, o_ref))


def _moba_attention(qkv):
    return _attn_call(
        _moba_kernel, (), (), qkv, (COL_QC, COL_KC, COL_VC), N_HEADS // 2,
        (pltpu.VMEM((2, SEQ - (MOBA_TOPK + 1) * MOBA_BLOCK, LANES), BF16),
         pltpu.VMEM((2, SEQ, LANES), BF16)),
        "moba_attention")


def _post_norm_residual(x, y, g, gate):
    ms = jnp.mean(y * y, axis=-1, keepdims=True)
    return x + gate * (y * lax.rsqrt(ms + RMS_EPS) * g)


def _merge_kernel(x_ref, mod_ref, g_ref, oa_ref, ob_ref, oc_ref, gate_ref,
                  wa_ref, wb_ref, wc_ref, wo_ref, o_ref):
    merged = None
    for j, (o_br, w_br) in enumerate(((oa_ref, wa_ref), (ob_ref, wb_ref), (oc_ref, wc_ref))):
        y = jnp.dot(o_br[...], w_br[...], preferred_element_type=F32)
        t = gate_ref[:, j * D_MODEL:(j + 1) * D_MODEL].astype(F32) * y
        merged = t if merged is None else merged + t
    y = jnp.dot(merged.astype(BF16), wo_ref[...], preferred_element_type=F32)
    o_ref[...] = _post_norm_residual(x_ref[...], y, g_ref[...], mod_ref[0, 2:3, :])


def _merge(x, mod, g_post, oa, ob, oc, gates, wa, wb, wc, wo, l):
    tok = lambda i: (i, 0)
    return pl.pallas_call(
        _merge_kernel,
        grid=(TOKENS // TM,),
        in_specs=[pl.BlockSpec((TM, D_MODEL), tok),
                  pl.BlockSpec((1, 6, D_MODEL), lambda i: (i // TILES_PER_BATCH, 0, 0)),
                  _resident((1, D_MODEL)),
                  pl.BlockSpec((TM, oa.shape[1]), tok),
                  pl.BlockSpec((TM, ob.shape[1]), tok),
                  pl.BlockSpec((TM, oc.shape[1]), tok),
                  pl.BlockSpec((TM, GATE_WIDTH), tok),
                  _resident_layer(wa.shape, l), _resident_layer(wb.shape, l),
                  _resident_layer(wc.shape, l), _resident_layer(wo.shape, l)],
        out_specs=pl.BlockSpec((TM, D_MODEL), tok),
        out_shape=jax.ShapeDtypeStruct((TOKENS, D_MODEL), F32),
        compiler_params=_params(),
        name="merge_out",
    )(x, mod, g_post, oa, ob, oc, gates, wa, wb, wc, wo)


FFN_CHUNKS = ((0, 1536), (1536, D_FF))


def _ffn_kernel(x_ref, mod_ref, gpre_ref, gpost_ref, wgu_ref, wd_ref, o_ref):
    for r0 in range(0, TM, SUB):
        x = x_ref[r0:r0 + SUB, :]
        hb = _modulated_norm(x, gpre_ref[...], mod_ref[0, 3:4, :], mod_ref[0, 4:5, :]).astype(BF16)
        y = None
        for c0, c1 in FFN_CHUNKS:
            a = jnp.dot(hb, wgu_ref[:, c0:c1], preferred_element_type=F32)
            b = jnp.dot(hb, wgu_ref[:, D_FF + c0:D_FF + c1], preferred_element_type=F32)
            act = (a * jax.nn.sigmoid(a) * b).astype(BF16)
            t = jnp.dot(act, wd_ref[c0:c1, :], preferred_element_type=F32)
            y = t if y is None else y + t
        o_ref[r0:r0 + SUB, :] = _post_norm_residual(x, y, gpost_ref[...], mod_ref[0, 5:6, :])


def _ffn(x, mod, g_pre, g_post, wgu, wd, l):
    tok = lambda i: (i, 0)
    return pl.pallas_call(
        _ffn_kernel,
        grid=(TOKENS // TM,),
        in_specs=[pl.BlockSpec((TM, D_MODEL), tok),
                  pl.BlockSpec((1, 6, D_MODEL), lambda i: (i // TILES_PER_BATCH, 0, 0)),
                  _resident((1, D_MODEL)), _resident((1, D_MODEL)),
                  _resident_layer(wgu.shape, l), _resident_layer(wd.shape, l)],
        out_specs=pl.BlockSpec((TM, D_MODEL), tok),
        out_shape=jax.ShapeDtypeStruct((TOKENS, D_MODEL), F32),
        compiler_params=_params(),
        name="swiglu",
    )(x, mod, g_pre, g_post, wgu, wd)


def kernel(x, c, positions, w_ada, b_ada, g_pre_mix, g_post_mix, w_in, b_fgt, lam_q1, lam_k1,
           lam_q2, lam_k2, g_subln, w_br_a, w_br_b, w_br_c, w_out, g_pre_ffn, g_post_ffn,
           w_gate_up, w_down):
    cos, sin = _rope_tables(positions)
    mod = _ada_mod(c, w_ada, b_ada).reshape(DEPTH, BATCH, 6, D_MODEL)

    scale = HEAD_DIM ** -0.5 * LOG2E
    col = jnp.arange(W_ALL)
    is_q = (col < 512) | ((col >= 1536) & (col < 1792)) | ((col >= 2304) & (col < 2560))
    w_all = _win_prep(w_in, jnp.where(is_q, scale, 1.0).astype(F32)[None, :])
    wa, wb, wc, wo = (w.astype(BF16) for w in (w_br_a, w_br_b, w_br_c, w_out))
    wgu, wd = w_gate_up.astype(BF16), w_down.astype(BF16)

    xt = x.reshape(TOKENS, D_MODEL)
    for l in range(DEPTH):
        lam_init = 0.8 - 0.6 * math.exp(-0.3 * l)
        bf = jnp.pad(b_fgt[l], (0, LANES - N_HEADS))[None, :]
        qkv, gates, logf = _inproj(xt, mod[l], g_pre_mix[l][None, :], w_all, l, bf, cos, sin)

        qkv3 = qkv.reshape(BATCH, SEQ, QKV_WIDTH)
        lam_params = jnp.stack([lam_q1[l], lam_k1[l], lam_q2[l], lam_k2[l]])
        oa = _diff_attention(qkv3, lam_params, g_subln[l][None, :], lam_init)
        ob = _fox_attention(qkv3, logf.reshape(BATCH, SEQ, LANES))
        oc = _moba_attention(qkv3)

        xt = _merge(xt, mod[l], g_post_mix[l][None, :],
                    oa.reshape(TOKENS, -1), ob.reshape(TOKENS, -1), oc.reshape(TOKENS, -1), gates,
                    wa, wb, wc, wo, l)
        xt = _ffn(xt, mod[l], g_pre_ffn[l][None, :], g_post_ffn[l][None, :], wgu, wd, l)
    return xt.reshape(BATCH, SEQ, D_MODEL)
```

```python
import functools
import math

import jax
import jax.numpy as jnp
from jax import lax
from jax.experimental import pallas as pl
from jax.experimental.pallas import tpu as pltpu

D_MODEL = 1024
BATCH = 16
SEQ = 2048
DEPTH = 2
HEAD_DIM = 64
N_HEADS = 4
QKV_WIDTH = 3072
GATE_WIDTH = 3 * D_MODEL
D_FF = 2816
ROPE_THETA = 10000.0
MOBA_BLOCK = 256
MOBA_TOPK = 3
RMS_EPS = 1e-6
NEG_INF = -1e30
LOG2E = math.log2(math.e)

LANES = 128
TOKENS = BATCH * SEQ
TM = 1024
TILES_PER_BATCH = SEQ // TM
SUB = 256
TQ = 256
NQ = SEQ // TQ
N_KV_BLOCKS = SEQ // MOBA_BLOCK
VMEM_LIMIT = 56 * 1024 * 1024

COL_QA, COL_KA, COL_VA = 0, 4, 8
COL_QB, COL_KB, COL_VB = 12, 14, 16
COL_QC, COL_KC, COL_VC = 18, 20, 22
QKV_SECTIONS = (
    (0, 512, True), (512, 512, True), (1024, 512, False),
    (1536, 256, False), (1792, 256, False), (2048, 256, False),
    (2304, 256, True), (2560, 256, True), (2816, 256, False),
)

F32 = jnp.float32
BF16 = jnp.bfloat16


def _params(**kw):
    return pltpu.CompilerParams(vmem_limit_bytes=VMEM_LIMIT, **kw)


def _resident(shape):
    nd = len(shape)
    return pl.BlockSpec(shape, lambda *_: (0,) * nd, pipeline_mode=pl.Buffered(1))


def _resident_layer(shape, l):
    return pl.BlockSpec((None,) + tuple(shape[1:]), lambda *_: (l, 0, 0), pipeline_mode=pl.Buffered(1))


def _dot_nt(a, b):
    return lax.dot_general(a, b, (((1,), (1,)), ((), ())), preferred_element_type=F32)


def _split3(x):
    hi = x.astype(BF16)
    r1 = x - hi.astype(F32)
    mid = r1.astype(BF16)
    lo = (r1 - mid.astype(F32)).astype(BF16)
    return hi, mid, lo


def _rope_table_kernel(pos_ref, invf_ref, sign_ref, cos_ref, sin_ref):
    ang = pos_ref[...].astype(F32) * invf_ref[...]
    cos_ref[...] = jnp.cos(ang)
    sin_ref[...] = jnp.sin(ang) * sign_ref[...]


def _rope_tables(positions):
    inv_freq = 1.0 / (ROPE_THETA ** (jnp.arange(0, HEAD_DIM, 2, dtype=F32) / HEAD_DIM))
    invf = jnp.tile(inv_freq, LANES // (HEAD_DIM // 2))[None, :]
    half = (jnp.arange(LANES) % HEAD_DIM) < HEAD_DIM // 2
    sign = jnp.where(half, -1.0, 1.0).astype(F32)[None, :]

    def tables(pos):
        n = pos.shape[0]
        return pl.pallas_call(
            _rope_table_kernel,
            grid=(n // SEQ,),
            in_specs=[pl.BlockSpec((SEQ, 1), lambda i: (i, 0)),
                      pl.BlockSpec((1, LANES), lambda i: (0, 0)),
                      pl.BlockSpec((1, LANES), lambda i: (0, 0))],
            out_specs=[pl.BlockSpec((SEQ, LANES), lambda i: (i, 0))] * 2,
            out_shape=[jax.ShapeDtypeStruct((n, LANES), F32)] * 2,
            compiler_params=_params(),
            name="rope_tables",
        )(pos.reshape(n, 1), invf, sign)

    def shared(pos):
        return tuple(jnp.tile(t, (BATCH, 1)) for t in tables(pos[0]))

    def per_row(pos):
        return tuple(tables(pos.reshape(TOKENS)))

    return lax.cond(jnp.all(positions == positions[:1]), shared, per_row, positions)


def _ada_kernel(c_ref, w_ref, b_ref, o_ref):
    c = c_ref[...]
    a = _split3(c * jax.nn.sigmoid(c))
    w = _split3(w_ref[0])
    acc = b_ref[0]
    for i, j in ((1, 0), (0, 1), (0, 0)):
        acc = acc + jnp.dot(a[i], w[j], preferred_element_type=F32)
    o_ref[0] = acc


def _ada_mod(c, w_ada, b_ada):
    tn = 1536
    n = 6 * D_MODEL
    return pl.pallas_call(
        _ada_kernel,
        grid=(DEPTH, n // tn),
        in_specs=[pl.BlockSpec((BATCH, D_MODEL), lambda l, j: (0, 0)),
                  pl.BlockSpec((1, D_MODEL, tn), lambda l, j: (l, 0, j)),
                  pl.BlockSpec((1, 1, tn), lambda l, j: (l, 0, j))],
        out_specs=pl.BlockSpec((1, BATCH, tn), lambda l, j: (l, 0, j)),
        out_shape=jax.ShapeDtypeStruct((DEPTH, BATCH, n), F32),
        compiler_params=_params(),
        name="ada_mod",
    )(c, w_ada, b_ada.reshape(DEPTH, 1, n))


D_IN = QKV_WIDTH + GATE_WIDTH + N_HEADS
W_ALL = QKV_WIDTH + GATE_WIDTH + LANES
W_BLOCK = 896


def _win_prep_kernel(w_ref, scale_ref, o_ref):
    col = pl.program_id(1) * W_BLOCK + lax.broadcasted_iota(jnp.int32, (1, W_BLOCK), 1)
    o_ref[0] = jnp.where(col < D_IN, w_ref[0] * scale_ref[...], 0.0).astype(BF16)


def _win_prep(w_in, col_scale):
    return pl.pallas_call(
        _win_prep_kernel,
        grid=(DEPTH, W_ALL // W_BLOCK),
        in_specs=[pl.BlockSpec((1, D_MODEL, W_BLOCK), lambda l, j: (l, 0, j)),
                  pl.BlockSpec((1, W_BLOCK), lambda l, j: (0, j))],
        out_specs=pl.BlockSpec((1, D_MODEL, W_BLOCK), lambda l, j: (l, 0, j)),
        out_shape=jax.ShapeDtypeStruct((DEPTH, D_MODEL, W_ALL), BF16),
        compiler_params=_params(),
        name="win_prep",
    )(w_in, col_scale)


def _modulated_norm(x, g, shift, scale):
    ms = jnp.mean(x * x, axis=-1, keepdims=True)
    return (x * lax.rsqrt(ms + RMS_EPS) * g) * (1.0 + scale) + shift


def _inproj_kernel(x_ref, mod_ref, g_ref, w_ref, bf_ref, cos_ref, sin_ref,
                   qkv_ref, gate_ref, logf_ref):
    lane = lax.broadcasted_iota(jnp.int32, (1, LANES), 1)
    first_half = (lane % HEAD_DIM) < HEAD_DIM // 2
    for r0 in range(0, TM, SUB):
        rows = slice(r0, r0 + SUB)
        h = _modulated_norm(x_ref[rows, :], g_ref[...], mod_ref[0, 0:1, :], mod_ref[0, 1:2, :])
        hb = h.astype(BF16)
        cos = cos_ref[rows, :]
        sin = sin_ref[rows, :]

        def rope(t):
            rot = jnp.where(first_half, pltpu.roll(t, LANES - HEAD_DIM // 2, 1),
                            pltpu.roll(t, HEAD_DIM // 2, 1))
            return t * cos + rot * sin

        for c0, width, roped in QKV_SECTIONS:
            acc = jnp.dot(hb, w_ref[:, c0:c0 + width], preferred_element_type=F32)
            if roped:
                for g in range(width // LANES):
                    qkv_ref[rows, c0 + g * LANES:c0 + (g + 1) * LANES] = rope(
                        acc[:, g * LANES:(g + 1) * LANES]).astype(BF16)
            else:
                qkv_ref[rows, c0:c0 + width] = acc.astype(BF16)

        for c0 in range(0, GATE_WIDTH, D_MODEL):
            acc = jnp.dot(hb, w_ref[:, QKV_WIDTH + c0:QKV_WIDTH + c0 + D_MODEL],
                          preferred_element_type=F32)
            gate_ref[rows, c0:c0 + D_MODEL] = jax.nn.sigmoid(acc).astype(BF16)

        z = jnp.dot(hb, w_ref[:, QKV_WIDTH + GATE_WIDTH:], preferred_element_type=F32) + bf_ref[...]
        logf_ref[rows, :] = jnp.minimum(z, 0.0) - jnp.log1p(jnp.exp(-jnp.abs(z)))


def _inproj(x, mod, g_pre, w_all, l, b_fgt, cos, sin):
    tok = lambda i: (i, 0)
    return pl.pallas_call(
        _inproj_kernel,
        grid=(TOKENS // TM,),
        in_specs=[pl.BlockSpec((TM, D_MODEL), tok),
                  pl.BlockSpec((1, 6, D_MODEL), lambda i: (i // TILES_PER_BATCH, 0, 0)),
                  _resident((1, D_MODEL)),
                  _resident_layer(w_all.shape, l),
                  _resident((1, LANES)),
                  pl.BlockSpec((TM, LANES), tok),
                  pl.BlockSpec((TM, LANES), tok)],
        out_specs=[pl.BlockSpec((TM, QKV_WIDTH), tok),
                   pl.BlockSpec((TM, GATE_WIDTH), tok),
                   pl.BlockSpec((TM, LANES), tok)],
        out_shape=[jax.ShapeDtypeStruct((TOKENS, QKV_WIDTH), BF16),
                   jax.ShapeDtypeStruct((TOKENS, GATE_WIDTH), BF16),
                   jax.ShapeDtypeStruct((TOKENS, LANES), F32)],
        compiler_params=_params(),
        name="inproj",
    )(x, mod, g_pre, w_all, b_fgt, cos, sin)


FAST_ROWS = 128
F32_HUGE = 3.0e38


def _causal(s, row_offset):
    row = lax.broadcasted_iota(jnp.int32, s.shape, 0) + row_offset
    col = lax.broadcasted_iota(jnp.int32, s.shape, 1)
    return jnp.where(col <= row, s, NEG_INF)


def _fast_unit(q, diag_bias, bias, k_ref, vaug_ref, i, row0):
    r = q.shape[0]
    m = jnp.sum(q.astype(F32) * k_ref[row0:row0 + r, :].astype(F32), axis=-1, keepdims=True)
    if diag_bias is not None:
        m = m + diag_bias
    acc = None
    for c in range(i + 1):
        s = _dot_nt(q, k_ref[c * TQ:(c + 1) * TQ, :])
        shift = m
        b = bias(c)
        if b is not None and b.shape[1] == 1:
            shift = m - b
        elif b is not None:
            s = s + b
        if c == i:
            s = _causal(s, row0 - i * TQ)
        p = jnp.exp2(s - shift).astype(BF16)
        t = jnp.dot(p, vaug_ref[c * TQ:(c + 1) * TQ, :], preferred_element_type=F32)
        acc = t if acc is None else acc + t
    return acc


def _normalised(o_aug):
    return o_aug[:, :LANES] / o_aug[:, LANES:]


def _attend_pairs(prepare, unit_terms, k_refs, vaug_ref, s_ref, p_ref, finish):
    overflowed = jnp.zeros((FAST_ROWS, 2 * LANES), F32)
    prepare(0)
    for i in range(NQ):
        if i + 1 < NQ:
            prepare(i + 1)
        for r0 in range(i * TQ, (i + 1) * TQ, FAST_ROWS):
            accs = [_fast_unit(*unit_terms(half, r0, FAST_ROWS), k_refs[half], vaug_ref, i, r0)
                    for half in (0, 1)]
            finite = jnp.abs(accs[0] + accs[1]) < F32_HUGE
            overflowed = overflowed + jnp.where(finite, 0.0, 1.0)
            finish(r0, _normalised(accs[0]), _normalised(accs[1]))

    @pl.when(jnp.sum(overflowed) > 0.0)
    def _():
        _attend_pairs_two_pass(unit_terms, k_refs, vaug_ref, s_ref, p_ref, finish)


def _scores(q, bias, k_ref, s_ref, i):
    for c in range(i + 1):
        s = _dot_nt(q, k_ref[c * TQ:(c + 1) * TQ, :])
        b = bias(c)
        if b is not None:
            s = s + b
        if c == i:
            s = _causal(s, 0)
        s_ref[:, c * TQ:(c + 1) * TQ] = s


def _softmax_pv(s_ref, p_ref, vaug_ref, i):
    w = (i + 1) * TQ
    n_groups = w // LANES
    mm = s_ref[:, 0:LANES]
    for g in range(1, n_groups):
        mm = jnp.maximum(mm, s_ref[:, g * LANES:(g + 1) * LANES])
    m = jnp.max(mm, axis=-1, keepdims=True)
    for g in range(n_groups):
        p = jnp.exp2(s_ref[:, g * LANES:(g + 1) * LANES] - m)
        p_ref[:, g * LANES:(g + 1) * LANES] = p.astype(BF16)
    return jnp.dot(p_ref[:, :w], vaug_ref[:w, :], preferred_element_type=F32)


def _attend_pairs_two_pass(unit_terms, k_refs, vaug_ref, s_ref, p_ref, finish):
    units = [(i, half) for i in range(NQ) for half in (0, 1)]

    def scores(n):
        i, half = units[n]
        q, _, bias = unit_terms(half, i * TQ, TQ)
        _scores(q, bias, k_refs[half], s_ref.at[half], i)

    scores(0)
    outs = [None, None]
    for n, (i, half) in enumerate(units):
        if n + 1 < len(units):
            scores(n + 1)
        outs[half] = _normalised(_softmax_pv(s_ref.at[half], p_ref.at[half], vaug_ref, i))
        if half == 1:
            finish(i * TQ, outs[0], outs[1])


def _fill_vaug(vaug_ref, v_ref, i):
    rows = slice(i * TQ, (i + 1) * TQ)
    vaug_ref[rows, :LANES] = v_ref[rows, :]
    vaug_ref[rows, LANES:] = jnp.ones((TQ, LANES), BF16)


def _head_masks(shape):
    lane = lax.broadcasted_iota(jnp.int32, shape, 1)
    return lane < HEAD_DIM, lane >= HEAD_DIM


def _masked_q(q_ref, half, row0, rows):
    q2 = q_ref[row0:row0 + rows, :]
    return jnp.where(_head_masks(q2.shape)[half], q2, jnp.zeros_like(q2))


def _attn_call(kernel, extra_inputs, extra_specs, qkv, cols, n_pairs, scratch, name):
    cq, ck, cv = cols
    blk = (None, SEQ, LANES)
    return pl.pallas_call(
        kernel,
        grid=(BATCH, n_pairs),
        in_specs=list(extra_specs) + [
            pl.BlockSpec(blk, lambda b, p: (b, 0, cq + p)),
            pl.BlockSpec(blk, lambda b, p: (b, 0, ck + p)),
            pl.BlockSpec(blk, lambda b, p: (b, 0, cv + p))],
        out_specs=pl.BlockSpec(blk, lambda b, p: (b, 0, p)),
        out_shape=jax.ShapeDtypeStruct((BATCH, SEQ, n_pairs * LANES), BF16),
        scratch_shapes=[pltpu.VMEM((2, TQ, SEQ), F32), pltpu.VMEM((2, TQ, SEQ), BF16),
                        pltpu.VMEM((SEQ, 2 * LANES), BF16)] + list(scratch),
        compiler_params=_params(),
        name=name,
    )(*extra_inputs, qkv, qkv, qkv)


def _write_heads(o_ref, row0, o1, o2):
    o_ref[row0:row0 + o1.shape[0], :] = jnp.where(_head_masks(o1.shape)[0], o1, o2).astype(BF16)


def _diff_kernel(lam_ref, gsub_ref, q_ref, k_ref, v_ref, o_ref, s_ref, p_ref, vaug_ref, *, lam_init):
    lp = lam_ref[...]
    lam = (jnp.exp(jnp.sum(lp[0:1] * lp[1:2], axis=-1, keepdims=True))
           - jnp.exp(jnp.sum(lp[2:3] * lp[3:4], axis=-1, keepdims=True)) + lam_init)

    def prepare(i):
        _fill_vaug(vaug_ref, v_ref, i)

    def unit_terms(half, row0, rows):
        return _masked_q(q_ref, half, row0, rows), None, lambda c: None

    def finish(row0, o1, o2):
        o = o1 - lam * o2
        ms = jnp.mean(o * o, axis=-1, keepdims=True)
        o = (o * lax.rsqrt(ms + RMS_EPS) * gsub_ref[...]) * (1.0 - lam_init)
        o_ref[row0:row0 + o.shape[0], :] = o.astype(BF16)

    _attend_pairs(prepare, unit_terms, (k_ref, k_ref), vaug_ref, s_ref, p_ref, finish)


def _diff_attention(qkv, lam_params, g_subln, lam_init):
    return _attn_call(
        functools.partial(_diff_kernel, lam_init=lam_init),
        (lam_params, g_subln),
        (pl.BlockSpec((4, HEAD_DIM), lambda b, p: (0, 0)),
         pl.BlockSpec((1, 2 * HEAD_DIM), lambda b, p: (0, 0))),
        qkv, (COL_QA, COL_KA, COL_VA), N_HEADS, (), "diff_attention")


def _fox_kernel(logf_ref, q_ref, k_ref, v_ref, o_ref, s_ref, p_ref, vaug_ref, cum_ref, nckt_ref):
    pair = pl.program_id(1)
    tri = (lax.broadcasted_iota(jnp.int32, (TQ, TQ), 0)
           >= lax.broadcasted_iota(jnp.int32, (TQ, TQ), 1)).astype(BF16)
    lane = lax.broadcasted_iota(jnp.int32, (1, LANES), 1)
    carry = [jnp.zeros((1, LANES), F32)]

    def prepare(i):
        rows = slice(i * TQ, (i + 1) * TQ)
        _fill_vaug(vaug_ref, v_ref, i)
        lf = logf_ref[rows, :]
        hi = lf.astype(BF16)
        lo = (lf - hi.astype(F32)).astype(BF16)
        r = jnp.dot(tri, jnp.concatenate([hi, lo], axis=1), preferred_element_type=F32)
        cum = (r[:, :LANES] + r[:, LANES:]) + carry[0]
        carry[0] = cum[TQ - 1:TQ, :]
        cum2 = cum * LOG2E
        cum_ref[rows, :] = cum2
        nckt_ref[:, rows] = jnp.transpose(-cum2)[:N_HEADS * 2, :]

    def unit_terms(half, row0, rows):
        head = 2 * pair + half
        own = jnp.where(lane == head, cum_ref[row0:row0 + rows, :], 0.0)
        diag_bias = -jnp.sum(own, axis=-1, keepdims=True)
        return (_masked_q(q_ref, half, row0, rows), diag_bias,
                lambda c: nckt_ref[pl.ds(head, 1), c * TQ:(c + 1) * TQ])

    _attend_pairs(prepare, unit_terms, (k_ref, k_ref), vaug_ref, s_ref, p_ref,
                  functools.partial(_write_heads, o_ref))


def _fox_attention(qkv, logf):
    return _attn_call(
        _fox_kernel, (logf,),
        (pl.BlockSpec((None, SEQ, LANES), lambda b, p: (b, 0, 0)),),
        qkv, (COL_QB, COL_KB, COL_VB), N_HEADS // 2,
        (pltpu.VMEM((SEQ, LANES), F32), pltpu.VMEM((2 * N_HEADS, SEQ), F32)),
        "fox_attention")


def _moba_kernel(q_ref, k_ref, v_ref, o_ref, s_ref, p_ref, vaug_ref, qhard_ref, kaug_ref):
    for i in range(NQ):
        _fill_vaug(vaug_ref, v_ref, i)
    member = (lax.broadcasted_iota(jnp.int32, (N_KV_BLOCKS, SEQ), 1) // MOBA_BLOCK
              == lax.broadcasted_iota(jnp.int32, (N_KV_BLOCKS, SEQ), 0)).astype(BF16)
    kmean = jnp.dot(member, k_ref[...], preferred_element_type=F32) * (1.0 / MOBA_BLOCK)
    km_hi = kmean.astype(BF16)
    km_lo = (kmean - km_hi.astype(F32)).astype(BF16)
    kmx = jnp.concatenate([km_hi, km_lo], axis=1)

    n_easy = (MOBA_TOPK + 1) * MOBA_BLOCK
    n_hard = SEQ - n_easy
    lane_row = lax.broadcasted_iota(jnp.int32, (1, LANES), 1)
    blk = lax.broadcasted_iota(jnp.int32, (N_KV_BLOCKS, n_hard), 0)
    blk_f = blk.astype(F32)
    past = blk < lax.broadcasted_iota(jnp.int32, (N_KV_BLOCKS, n_hard), 1) // MOBA_BLOCK + (MOBA_TOPK + 1)
    for half in (0, 1):
        base = HEAD_DIM * (1 - half)
        own = _head_masks((TQ, LANES))[half]
        for n in range(N_KV_BLOCKS):
            rows = slice(n * MOBA_BLOCK, (n + 1) * MOBA_BLOCK)
            one_hot = jnp.where(lane_row == base + n, 1.0, 0.0).astype(BF16)
            kaug_ref[half, rows, :] = jnp.where(own, k_ref[rows, :], one_hot)
        qh = _masked_q(q_ref, half, n_easy, n_hard)
        gate = _dot_nt(kmx, jnp.concatenate([qh, qh], axis=1))
        gate = jnp.where(past, gate, NEG_INF)
        drop = jnp.where(past, -1.0, 0.0)
        for _ in range(MOBA_TOPK):
            top = jnp.max(gate, axis=0, keepdims=True)
            idx = jnp.min(jnp.where(gate == top, blk_f, float(N_KV_BLOCKS)), axis=0, keepdims=True)
            hit = blk_f == idx
            drop = jnp.where(hit, 0.0, drop)
            gate = jnp.where(hit, NEG_INF, gate)
        place = (lax.broadcasted_iota(jnp.int32, (N_KV_BLOCKS, LANES), 1)
                 == lax.broadcasted_iota(jnp.int32, (N_KV_BLOCKS, LANES), 0) + base).astype(BF16)
        dropped = lax.dot_general(drop.astype(BF16), place, (((0,), (0,)), ((), ())),
                                  preferred_element_type=F32)
        own_h = _head_masks((n_hard, LANES))[half]
        qhard_ref[half] = jnp.where(own_h, q_ref[n_easy:, :], (dropped * -NEG_INF).astype(BF16))

    def unit_terms(half, row0, rows):
        if row0 < n_easy:
            return _masked_q(q_ref, half, row0, rows), None, lambda c: None
        return qhard_ref[half, row0 - n_easy:row0 - n_easy + rows, :], None, lambda c: None

    _attend_pairs(lambda i: None, unit_terms, (kaug_ref.at[0], kaug_ref.at[1]), vaug_ref, s_ref, p_ref,
                  functools.partial(_write_heads, o_ref))


def _moba_attention(qkv):
    return _attn_call(
        _moba_kernel, (), (), qkv, (COL_QC, COL_KC, COL_VC), N_HEADS // 2,
        (pltpu.VMEM((2, SEQ - (MOBA_TOPK + 1) * MOBA_BLOCK, LANES), BF16),
         pltpu.VMEM((2, SEQ, LANES), BF16)),
        "moba_attention")


def _post_norm_residual(x, y, g, gate):
    ms = jnp.mean(y * y, axis=-1, keepdims=True)
    return x + gate * (y * lax.rsqrt(ms + RMS_EPS) * g)


def _merge_kernel(x_ref, mod_ref, g_ref, oa_ref, ob_ref, oc_ref, gate_ref,
                  wa_ref, wb_ref, wc_ref, wo_ref, o_ref):
    merged = None
    for j, (o_br, w_br) in enumerate(((oa_ref, wa_ref), (ob_ref, wb_ref), (oc_ref, wc_ref))):
        y = jnp.dot(o_br[...], w_br[...], preferred_element_type=F32)
        t = gate_ref[:, j * D_MODEL:(j + 1) * D_MODEL].astype(F32) * y
        merged = t if merged is None else merged + t
    y = jnp.dot(merged.astype(BF16), wo_ref[...], preferred_element_type=F32)
    o_ref[...] = _post_norm_residual(x_ref[...], y, g_ref[...], mod_ref[0, 2:3, :])


def _merge(x, mod, g_post, oa, ob, oc, gates, wa, wb, wc, wo, l):
    tok = lambda i: (i, 0)
    return pl.pallas_call(
        _merge_kernel,
        grid=(TOKENS // TM,),
        in_specs=[pl.BlockSpec((TM, D_MODEL), tok),
                  pl.BlockSpec((1, 6, D_MODEL), lambda i: (i // TILES_PER_BATCH, 0, 0)),
                  _resident((1, D_MODEL)),
                  pl.BlockSpec((TM, oa.shape[1]), tok),
                  pl.BlockSpec((TM, ob.shape[1]), tok),
                  pl.BlockSpec((TM, oc.shape[1]), tok),
                  pl.BlockSpec((TM, GATE_WIDTH), tok),
                  _resident_layer(wa.shape, l), _resident_layer(wb.shape, l),
                  _resident_layer(wc.shape, l), _resident_layer(wo.shape, l)],
        out_specs=pl.BlockSpec((TM, D_MODEL), tok),
        out_shape=jax.ShapeDtypeStruct((TOKENS, D_MODEL), F32),
        compiler_params=_params(),
        name="merge_out",
    )(x, mod, g_post, oa, ob, oc, gates, wa, wb, wc, wo)


FFN_CHUNKS = ((0, 1536), (1536, D_FF))


def _ffn_kernel(x_ref, mod_ref, gpre_ref, gpost_ref, wgu_ref, wd_ref, o_ref):
    for r0 in range(0, TM, SUB):
        x = x_ref[r0:r0 + SUB, :]
        hb = _modulated_norm(x, gpre_ref[...], mod_ref[0, 3:4, :], mod_ref[0, 4:5, :]).astype(BF16)
        y = None
        for c0, c1 in FFN_CHUNKS:
            a = jnp.dot(hb, wgu_ref[:, c0:c1], preferred_element_type=F32)
            b = jnp.dot(hb, wgu_ref[:, D_FF + c0:D_FF + c1], preferred_element_type=F32)
            act = (a * jax.nn.sigmoid(a) * b).astype(BF16)
            t = jnp.dot(act, wd_ref[c0:c1, :], preferred_element_type=F32)
            y = t if y is None else y + t
        o_ref[r0:r0 + SUB, :] = _post_norm_residual(x, y, gpost_ref[...], mod_ref[0, 5:6, :])


def _ffn(x, mod, g_pre, g_post, wgu, wd, l):
    tok = lambda i: (i, 0)
    return pl.pallas_call(
        _ffn_kernel,
        grid=(TOKENS // TM,),
        in_specs=[pl.BlockSpec((TM, D_MODEL), tok),
                  pl.BlockSpec((1, 6, D_MODEL), lambda i: (i // TILES_PER_BATCH, 0, 0)),
                  _resident((1, D_MODEL)), _resident((1, D_MODEL)),
                  _resident_layer(wgu.shape, l), _resident_layer(wd.shape, l)],
        out_specs=pl.BlockSpec((TM, D_MODEL), tok),
        out_shape=jax.ShapeDtypeStruct((TOKENS, D_MODEL), F32),
        compiler_params=_params(),
        name="swiglu",
    )(x, mod, g_pre, g_post, wgu, wd)


def kernel(x, c, positions, w_ada, b_ada, g_pre_mix, g_post_mix, w_in, b_fgt, lam_q1, lam_k1,
           lam_q2, lam_k2, g_subln, w_br_a, w_br_b, w_br_c, w_out, g_pre_ffn, g_post_ffn,
           w_gate_up, w_down):
    cos, sin = _rope_tables(positions)
    mod = _ada_mod(c, w_ada, b_ada).reshape(DEPTH, BATCH, 6, D_MODEL)

    scale = HEAD_DIM ** -0.5 * LOG2E
    col = jnp.arange(W_ALL)
    is_q = (col < 512) | ((col >= 1536) & (col < 1792)) | ((col >= 2304) & (col < 2560))
    w_all = _win_prep(w_in, jnp.where(is_q, scale, 1.0).astype(F32)[None, :])
    wa, wb, wc, wo = (w.astype(BF16) for w in (w_br_a, w_br_b, w_br_c, w_out))
    wgu, wd = w_gate_up.astype(BF16), w_down.astype(BF16)

    xt = x.reshape(TOKENS, D_MODEL)
    for l in range(DEPTH):
        lam_init = 0.8 - 0.6 * math.exp(-0.3 * l)
        bf = jnp.pad(b_fgt[l], (0, LANES - N_HEADS))[None, :]
        qkv, gates, logf = _inproj(xt, mod[l], g_pre_mix[l][None, :], w_all, l, bf, cos, sin)

        qkv3 = qkv.reshape(BATCH, SEQ, QKV_WIDTH)
        lam_params = jnp.stack([lam_q1[l], lam_k1[l], lam_q2[l], lam_k2[l]])
        oa = _diff_attention(qkv3, lam_params, g_subln[l][None, :], lam_init)
        ob = _fox_attention(qkv3, logf.reshape(BATCH, SEQ, LANES))
        oc = _moba_attention(qkv3)

        xt = _merge(xt, mod[l], g_post_mix[l][None, :],
                    oa.reshape(TOKENS, -1), ob.reshape(TOKENS, -1), oc.reshape(TOKENS, -1), gates,
                    wa, wb, wc, wo, l)
        xt = _ffn(xt, mod[l], g_pre_ffn[l][None, :], g_post_ffn[l][None, :], wgu, wd, l)
    return xt.reshape(BATCH, SEQ, D_MODEL)
```

```python
import functools
import math

import jax
import jax.numpy as jnp
from jax import lax
from jax.experimental import pallas as pl
from jax.experimental.pallas import tpu as pltpu

D_MODEL = 1024
BATCH = 16
SEQ = 2048
DEPTH = 2
HEAD_DIM = 64
N_HEADS = 4
QKV_WIDTH = 3072
GATE_WIDTH = 3 * D_MODEL
D_FF = 2816
ROPE_THETA = 10000.0
MOBA_BLOCK = 256
MOBA_TOPK = 3
RMS_EPS = 1e-6
NEG_INF = -1e30
LOG2E = math.log2(math.e)

LANES = 128
TOKENS = BATCH * SEQ
TM = 1024
TILES_PER_BATCH = SEQ // TM
SUB = 256
TQ = 256
NQ = SEQ // TQ
N_KV_BLOCKS = SEQ // MOBA_BLOCK
VMEM_LIMIT = 56 * 1024 * 1024

COL_QA, COL_KA, COL_VA = 0, 4, 8
COL_QB, COL_KB, COL_VB = 12, 14, 16
COL_QC, COL_KC, COL_VC = 18, 20, 22
QKV_SECTIONS = (
    (0, 512, True), (512, 512, True), (1024, 512, False),
    (1536, 256, False), (1792, 256, False), (2048, 256, False),
    (2304, 256, True), (2560, 256, True), (2816, 256, False),
)

F32 = jnp.float32
BF16 = jnp.bfloat16


def _params(**kw):
    return pltpu.CompilerParams(vmem_limit_bytes=VMEM_LIMIT, **kw)


def _resident(shape):
    nd = len(shape)
    return pl.BlockSpec(shape, lambda *_: (0,) * nd, pipeline_mode=pl.Buffered(1))


def _resident_layer(shape, l):
    return pl.BlockSpec((None,) + tuple(shape[1:]), lambda *_: (l, 0, 0), pipeline_mode=pl.Buffered(1))


def _dot_nt(a, b):
    return lax.dot_general(a, b, (((1,), (1,)), ((), ())), preferred_element_type=F32)


def _split3(x):
    hi = x.astype(BF16)
    r1 = x - hi.astype(F32)
    mid = r1.astype(BF16)
    lo = (r1 - mid.astype(F32)).astype(BF16)
    return hi, mid, lo


def _rope_table_kernel(pos_ref, invf_ref, sign_ref, cos_ref, sin_ref):
    ang = pos_ref[...].astype(F32) * invf_ref[...]
    cos_ref[...] = jnp.cos(ang)
    sin_ref[...] = jnp.sin(ang) * sign_ref[...]


def _rope_tables(positions):
    inv_freq = 1.0 / (ROPE_THETA ** (jnp.arange(0, HEAD_DIM, 2, dtype=F32) / HEAD_DIM))
    invf = jnp.tile(inv_freq, LANES // (HEAD_DIM // 2))[None, :]
    half = (jnp.arange(LANES) % HEAD_DIM) < HEAD_DIM // 2
    sign = jnp.where(half, -1.0, 1.0).astype(F32)[None, :]

    def tables(pos):
        n = pos.shape[0]
        return pl.pallas_call(
            _rope_table_kernel,
            grid=(n // SEQ,),
            in_specs=[pl.BlockSpec((SEQ, 1), lambda i: (i, 0)),
                      pl.BlockSpec((1, LANES), lambda i: (0, 0)),
                      pl.BlockSpec((1, LANES), lambda i: (0, 0))],
            out_specs=[pl.BlockSpec((SEQ, LANES), lambda i: (i, 0))] * 2,
            out_shape=[jax.ShapeDtypeStruct((n, LANES), F32)] * 2,
            compiler_params=_params(),
            name="rope_tables",
        )(pos.reshape(n, 1), invf, sign)

    def shared(pos):
        return tuple(jnp.tile(t, (BATCH, 1)) for t in tables(pos[0]))

    def per_row(pos):
        return tuple(tables(pos.reshape(TOKENS)))

    return lax.cond(jnp.all(positions == positions[:1]), shared, per_row, positions)


def _ada_kernel(c_ref, w_ref, b_ref, o_ref):
    c = c_ref[...]
    a = _split3(c * jax.nn.sigmoid(c))
    w = _split3(w_ref[0])
    acc = b_ref[0]
    for i, j in ((1, 0), (0, 1), (0, 0)):
        acc = acc + jnp.dot(a[i], w[j], preferred_element_type=F32)
    o_ref[0] = acc


def _ada_mod(c, w_ada, b_ada):
    tn = 1536
    n = 6 * D_MODEL
    return pl.pallas_call(
        _ada_kernel,
        grid=(DEPTH, n // tn),
        in_specs=[pl.BlockSpec((BATCH, D_MODEL), lambda l, j: (0, 0)),
                  pl.BlockSpec((1, D_MODEL, tn), lambda l, j: (l, 0, j)),
                  pl.BlockSpec((1, 1, tn), lambda l, j: (l, 0, j))],
        out_specs=pl.BlockSpec((1, BATCH, tn), lambda l, j: (l, 0, j)),
        out_shape=jax.ShapeDtypeStruct((DEPTH, BATCH, n), F32),
        compiler_params=_params(),
        name="ada_mod",
    )(c, w_ada, b_ada.reshape(DEPTH, 1, n))


D_IN = QKV_WIDTH + GATE_WIDTH + N_HEADS
W_ALL = QKV_WIDTH + GATE_WIDTH + LANES
W_BLOCK = 896


def _win_prep_kernel(w_ref, scale_ref, o_ref):
    col = pl.program_id(1) * W_BLOCK + lax.broadcasted_iota(jnp.int32, (1, W_BLOCK), 1)
    o_ref[0] = jnp.where(col < D_IN, w_ref[0] * scale_ref[...], 0.0).astype(BF16)


def _win_prep(w_in, col_scale):
    return pl.pallas_call(
        _win_prep_kernel,
        grid=(DEPTH, W_ALL // W_BLOCK),
        in_specs=[pl.BlockSpec((1, D_MODEL, W_BLOCK), lambda l, j: (l, 0, j)),
                  pl.BlockSpec((1, W_BLOCK), lambda l, j: (0, j))],
        out_specs=pl.BlockSpec((1, D_MODEL, W_BLOCK), lambda l, j: (l, 0, j)),
        out_shape=jax.ShapeDtypeStruct((DEPTH, D_MODEL, W_ALL), BF16),
        compiler_params=_params(),
        name="win_prep",
    )(w_in, col_scale)


def _modulated_norm(x, g, shift, scale):
    ms = jnp.mean(x * x, axis=-1, keepdims=True)
    return (x * lax.rsqrt(ms + RMS_EPS) * g) * (1.0 + scale) + shift


def _inproj_kernel(x_ref, mod_ref, g_ref, w_ref, bf_ref, cos_ref, sin_ref,
                   qkv_ref, gate_ref, logf_ref):
    lane = lax.broadcasted_iota(jnp.int32, (1, LANES), 1)
    first_half = (lane % HEAD_DIM) < HEAD_DIM // 2
    for r0 in range(0, TM, SUB):
        rows = slice(r0, r0 + SUB)
        h = _modulated_norm(x_ref[rows, :], g_ref[...], mod_ref[0, 0:1, :], mod_ref[0, 1:2, :])
        hb = h.astype(BF16)
        cos = cos_ref[rows, :]
        sin = sin_ref[rows, :]

        def rope(t):
            rot = jnp.where(first_half, pltpu.roll(t, LANES - HEAD_DIM // 2, 1),
                            pltpu.roll(t, HEAD_DIM // 2, 1))
            return t * cos + rot * sin

        for c0, width, roped in QKV_SECTIONS:
            acc = jnp.dot(hb, w_ref[:, c0:c0 + width], preferred_element_type=F32)
            if roped:
                for g in range(width // LANES):
                    qkv_ref[rows, c0 + g * LANES:c0 + (g + 1) * LANES] = rope(
                        acc[:, g * LANES:(g + 1) * LANES]).astype(BF16)
            else:
                qkv_ref[rows, c0:c0 + width] = acc.astype(BF16)

        for c0 in range(0, GATE_WIDTH, D_MODEL):
            acc = jnp.dot(hb, w_ref[:, QKV_WIDTH + c0:QKV_WIDTH + c0 + D_MODEL],
                          preferred_element_type=F32)
            gate_ref[rows, c0:c0 + D_MODEL] = jax.nn.sigmoid(acc).astype(BF16)

        z = jnp.dot(hb, w_ref[:, QKV_WIDTH + GATE_WIDTH:], preferred_element_type=F32) + bf_ref[...]
        logf_ref[rows, :] = jnp.minimum(z, 0.0) - jnp.log1p(jnp.exp(-jnp.abs(z)))


def _inproj(x, mod, g_pre, w_all, l, b_fgt, cos, sin):
    tok = lambda i: (i, 0)
    return pl.pallas_call(
        _inproj_kernel,
        grid=(TOKENS // TM,),
        in_specs=[pl.BlockSpec((TM, D_MODEL), tok),
                  pl.BlockSpec((1, 6, D_MODEL), lambda i: (i // TILES_PER_BATCH, 0, 0)),
                  _resident((1, D_MODEL)),
                  _resident_layer(w_all.shape, l),
                  _resident((1, LANES)),
                  pl.BlockSpec((TM, LANES), tok),
                  pl.BlockSpec((TM, LANES), tok)],
        out_specs=[pl.BlockSpec((TM, QKV_WIDTH), tok),
                   pl.BlockSpec((TM, GATE_WIDTH), tok),
                   pl.BlockSpec((TM, LANES), tok)],
        out_shape=[jax.ShapeDtypeStruct((TOKENS, QKV_WIDTH), BF16),
                   jax.ShapeDtypeStruct((TOKENS, GATE_WIDTH), BF16),
                   jax.ShapeDtypeStruct((TOKENS, LANES), F32)],
        compiler_params=_params(),
        name="inproj",
    )(x, mod, g_pre, w_all, b_fgt, cos, sin)


FAST_ROWS = 256
F32_HUGE = 3.0e38


def _causal(s, row_offset):
    row = lax.broadcasted_iota(jnp.int32, s.shape, 0) + row_offset
    col = lax.broadcasted_iota(jnp.int32, s.shape, 1)
    return jnp.where(col <= row, s, NEG_INF)


def _fast_unit(q, diag_bias, bias, k_ref, vaug_ref, i, row0):
    r = q.shape[0]
    m = jnp.sum(q.astype(F32) * k_ref[row0:row0 + r, :].astype(F32), axis=-1, keepdims=True)
    if diag_bias is not None:
        m = m + diag_bias
    acc = None
    for c in range(i + 1):
        s = _dot_nt(q, k_ref[c * TQ:(c + 1) * TQ, :])
        shift = m
        b = bias(c)
        if b is not None and b.shape[1] == 1:
            shift = m - b
        elif b is not None:
            s = s + b
        if c == i:
            s = _causal(s, row0 - i * TQ)
        p = jnp.exp2(s - shift).astype(BF16)
        t = jnp.dot(p, vaug_ref[c * TQ:(c + 1) * TQ, :], preferred_element_type=F32)
        acc = t if acc is None else acc + t
    return acc


def _normalised(o_aug):
    return o_aug[:, :LANES] / o_aug[:, LANES:]


def _attend_pairs(prepare, unit_terms, k_refs, vaug_ref, s_ref, p_ref, finish):
    overflowed = jnp.zeros((FAST_ROWS, 2 * LANES), F32)
    prepare(0)
    for i in range(NQ):
        if i + 1 < NQ:
            prepare(i + 1)
        for r0 in range(i * TQ, (i + 1) * TQ, FAST_ROWS):
            accs = [_fast_unit(*unit_terms(half, r0, FAST_ROWS), k_refs[half], vaug_ref, i, r0)
                    for half in (0, 1)]
            finite = jnp.abs(accs[0] + accs[1]) < F32_HUGE
            overflowed = overflowed + jnp.where(finite, 0.0, 1.0)
            finish(r0, _normalised(accs[0]), _normalised(accs[1]))

    @pl.when(jnp.sum(overflowed) > 0.0)
    def _():
        _attend_pairs_two_pass(unit_terms, k_refs, vaug_ref, s_ref, p_ref, finish)


def _scores(q, bias, k_ref, s_ref, i):
    for c in range(i + 1):
        s = _dot_nt(q, k_ref[c * TQ:(c + 1) * TQ, :])
        b = bias(c)
        if b is not None:
            s = s + b
        if c == i:
            s = _causal(s, 0)
        s_ref[:, c * TQ:(c + 1) * TQ] = s


def _softmax_pv(s_ref, p_ref, vaug_ref, i):
    w = (i + 1) * TQ
    n_groups = w // LANES
    mm = s_ref[:, 0:LANES]
    for g in range(1, n_groups):
        mm = jnp.maximum(mm, s_ref[:, g * LANES:(g + 1) * LANES])
    m = jnp.max(mm, axis=-1, keepdims=True)
    for g in range(n_groups):
        p = jnp.exp2(s_ref[:, g * LANES:(g + 1) * LANES] - m)
        p_ref[:, g * LANES:(g + 1) * LANES] = p.astype(BF16)
    return jnp.dot(p_ref[:, :w], vaug_ref[:w, :], preferred_element_type=F32)


def _attend_pairs_two_pass(unit_terms, k_refs, vaug_ref, s_ref, p_ref, finish):
    units = [(i, half) for i in range(NQ) for half in (0, 1)]

    def scores(n):
        i, half = units[n]
        q, _, bias = unit_terms(half, i * TQ, TQ)
        _scores(q, bias, k_refs[half], s_ref.at[half], i)

    scores(0)
    outs = [None, None]
    for n, (i, half) in enumerate(units):
        if n + 1 < len(units):
            scores(n + 1)
        outs[half] = _normalised(_softmax_pv(s_ref.at[half], p_ref.at[half], vaug_ref, i))
        if half == 1:
            finish(i * TQ, outs[0], outs[1])


def _fill_vaug(vaug_ref, v_ref, i):
    rows = slice(i * TQ, (i + 1) * TQ)
    vaug_ref[rows, :LANES] = v_ref[rows, :]
    vaug_ref[rows, LANES:] = jnp.ones((TQ, LANES), BF16)


def _head_masks(shape):
    lane = lax.broadcasted_iota(jnp.int32, shape, 1)
    return lane < HEAD_DIM, lane >= HEAD_DIM


def _masked_q(q_ref, half, row0, rows):
    q2 = q_ref[row0:row0 + rows, :]
    return jnp.where(_head_masks(q2.shape)[half], q2, jnp.zeros_like(q2))


def _attn_call(kernel, extra_inputs, extra_specs, qkv, cols, n_pairs, scratch, name):
    cq, ck, cv = cols
    blk = (None, SEQ, LANES)
    return pl.pallas_call(
        kernel,
        grid=(BATCH, n_pairs),
        in_specs=list(extra_specs) + [
            pl.BlockSpec(blk, lambda b, p: (b, 0, cq + p)),
            pl.BlockSpec(blk, lambda b, p: (b, 0, ck + p)),
            pl.BlockSpec(blk, lambda b, p: (b, 0, cv + p))],
        out_specs=pl.BlockSpec(blk, lambda b, p: (b, 0, p)),
        out_shape=jax.ShapeDtypeStruct((BATCH, SEQ, n_pairs * LANES), BF16),
        scratch_shapes=[pltpu.VMEM((2, TQ, SEQ), F32), pltpu.VMEM((2, TQ, SEQ), BF16),
                        pltpu.VMEM((SEQ, 2 * LANES), BF16)] + list(scratch),
        compiler_params=_params(),
        name=name,
    )(*extra_inputs, qkv, qkv, qkv)


def _write_heads(o_ref, row0, o1, o2):
    o_ref[row0:row0 + o1.shape[0], :] = jnp.where(_head_masks(o1.shape)[0], o1, o2).astype(BF16)


def _diff_kernel(lam_ref, gsub_ref, q_ref, k_ref, v_ref, o_ref, s_ref, p_ref, vaug_ref, *, lam_init):
    lp = lam_ref[...]
    lam = (jnp.exp(jnp.sum(lp[0:1] * lp[1:2], axis=-1, keepdims=True))
           - jnp.exp(jnp.sum(lp[2:3] * lp[3:4], axis=-1, keepdims=True)) + lam_init)

    def prepare(i):
        _fill_vaug(vaug_ref, v_ref, i)

    def unit_terms(half, row0, rows):
        return _masked_q(q_ref, half, row0, rows), None, lambda c: None

    def finish(row0, o1, o2):
        o = o1 - lam * o2
        ms = jnp.mean(o * o, axis=-1, keepdims=True)
        o = (o * lax.rsqrt(ms + RMS_EPS) * gsub_ref[...]) * (1.0 - lam_init)
        o_ref[row0:row0 + o.shape[0], :] = o.astype(BF16)

    _attend_pairs(prepare, unit_terms, (k_ref, k_ref), vaug_ref, s_ref, p_ref, finish)


def _diff_attention(qkv, lam_params, g_subln, lam_init):
    return _attn_call(
        functools.partial(_diff_kernel, lam_init=lam_init),
        (lam_params, g_subln),
        (pl.BlockSpec((4, HEAD_DIM), lambda b, p: (0, 0)),
         pl.BlockSpec((1, 2 * HEAD_DIM), lambda b, p: (0, 0))),
        qkv, (COL_QA, COL_KA, COL_VA), N_HEADS, (), "diff_attention")


def _fox_kernel(logf_ref, q_ref, k_ref, v_ref, o_ref, s_ref, p_ref, vaug_ref, cum_ref, nckt_ref):
    pair = pl.program_id(1)
    tri = (lax.broadcasted_iota(jnp.int32, (TQ, TQ), 0)
           >= lax.broadcasted_iota(jnp.int32, (TQ, TQ), 1)).astype(BF16)
    lane = lax.broadcasted_iota(jnp.int32, (1, LANES), 1)
    carry = [jnp.zeros((1, LANES), F32)]

    def prepare(i):
        rows = slice(i * TQ, (i + 1) * TQ)
        _fill_vaug(vaug_ref, v_ref, i)
        lf = logf_ref[rows, :]
        hi = lf.astype(BF16)
        lo = (lf - hi.astype(F32)).astype(BF16)
        r = jnp.dot(tri, jnp.concatenate([hi, lo], axis=1), preferred_element_type=F32)
        cum = (r[:, :LANES] + r[:, LANES:]) + carry[0]
        carry[0] = cum[TQ - 1:TQ, :]
        cum2 = cum * LOG2E
        cum_ref[rows, :] = cum2
        nckt_ref[:, rows] = jnp.transpose(-cum2)[:N_HEADS * 2, :]

    def unit_terms(half, row0, rows):
        head = 2 * pair + half
        own = jnp.where(lane == head, cum_ref[row0:row0 + rows, :], 0.0)
        diag_bias = -jnp.sum(own, axis=-1, keepdims=True)
        return (_masked_q(q_ref, half, row0, rows), diag_bias,
                lambda c: nckt_ref[pl.ds(head, 1), c * TQ:(c + 1) * TQ])

    _attend_pairs(prepare, unit_terms, (k_ref, k_ref), vaug_ref, s_ref, p_ref,
                  functools.partial(_write_heads, o_ref))


def _fox_attention(qkv, logf):
    return _attn_call(
        _fox_kernel, (logf,),
        (pl.BlockSpec((None, SEQ, LANES), lambda b, p: (b, 0, 0)),),
        qkv, (COL_QB, COL_KB, COL_VB), N_HEADS // 2,
        (pltpu.VMEM((SEQ, LANES), F32), pltpu.VMEM((2 * N_HEADS, SEQ), F32)),
        "fox_attention")


def _moba_kernel(q_ref, k_ref, v_ref, o_ref, s_ref, p_ref, vaug_ref, qhard_ref, kaug_ref):
    for i in range(NQ):
        _fill_vaug(vaug_ref, v_ref, i)
    member = (lax.broadcasted_iota(jnp.int32, (N_KV_BLOCKS, SEQ), 1) // MOBA_BLOCK
              == lax.broadcasted_iota(jnp.int32, (N_KV_BLOCKS, SEQ), 0)).astype(BF16)
    kmean = jnp.dot(member, k_ref[...], preferred_element_type=F32) * (1.0 / MOBA_BLOCK)
    km_hi = kmean.astype(BF16)
    km_lo = (kmean - km_hi.astype(F32)).astype(BF16)
    kmx = jnp.concatenate([km_hi, km_lo], axis=1)

    n_easy = (MOBA_TOPK + 1) * MOBA_BLOCK
    n_hard = SEQ - n_easy
    lane_row = lax.broadcasted_iota(jnp.int32, (1, LANES), 1)
    blk = lax.broadcasted_iota(jnp.int32, (N_KV_BLOCKS, n_hard), 0)
    blk_f = blk.astype(F32)
    past = blk < lax.broadcasted_iota(jnp.int32, (N_KV_BLOCKS, n_hard), 1) // MOBA_BLOCK + (MOBA_TOPK + 1)
    for half in (0, 1):
        base = HEAD_DIM * (1 - half)
        own = _head_masks((TQ, LANES))[half]
        for n in range(N_KV_BLOCKS):
            rows = slice(n * MOBA_BLOCK, (n + 1) * MOBA_BLOCK)
            one_hot = jnp.where(lane_row == base + n, 1.0, 0.0).astype(BF16)
            kaug_ref[half, rows, :] = jnp.where(own, k_ref[rows, :], one_hot)
        qh = _masked_q(q_ref, half, n_easy, n_hard)
        gate = _dot_nt(kmx, jnp.concatenate([qh, qh], axis=1))
        gate = jnp.where(past, gate, NEG_INF)
        drop = jnp.where(past, -1.0, 0.0)
        for _ in range(MOBA_TOPK):
            top = jnp.max(gate, axis=0, keepdims=True)
            idx = jnp.min(jnp.where(gate == top, blk_f, float(N_KV_BLOCKS)), axis=0, keepdims=True)
            hit = blk_f == idx
            drop = jnp.where(hit, 0.0, drop)
            gate = jnp.where(hit, NEG_INF, gate)
        place = (lax.broadcasted_iota(jnp.int32, (N_KV_BLOCKS, LANES), 1)
                 == lax.broadcasted_iota(jnp.int32, (N_KV_BLOCKS, LANES), 0) + base).astype(BF16)
        dropped = lax.dot_general(drop.astype(BF16), place, (((0,), (0,)), ((), ())),
                                  preferred_element_type=F32)
        own_h = _head_masks((n_hard, LANES))[half]
        qhard_ref[half] = jnp.where(own_h, q_ref[n_easy:, :], (dropped * -NEG_INF).astype(BF16))

    def unit_terms(half, row0, rows):
        if row0 < n_easy:
            return _masked_q(q_ref, half, row0, rows), None, lambda c: None
        return qhard_ref[half, row0 - n_easy:row0 - n_easy + rows, :], None, lambda c: None

    _attend_pairs(lambda i: None, unit_terms, (kaug_ref.at[0], kaug_ref.at[1]), vaug_ref, s_ref, p_ref,
                  functools.partial(_write_heads, o_ref))


def _moba_attention(qkv):
    return _attn_call(
        _moba_kernel, (), (), qkv, (COL_QC, COL_KC, COL_VC), N_HEADS // 2,
        (pltpu.VMEM((2, SEQ - (MOBA_TOPK + 1) * MOBA_BLOCK, LANES), BF16),
         pltpu.VMEM((2, SEQ, LANES), BF16)),
        "moba_attention")


def _post_norm_residual(x, y, g, gate):
    ms = jnp.mean(y * y, axis=-1, keepdims=True)
    return x + gate * (y * lax.rsqrt(ms + RMS_EPS) * g)


def _merge_kernel(x_ref, mod_ref, g_ref, oa_ref, ob_ref, oc_ref, gate_ref,
                  wa_ref, wb_ref, wc_ref, wo_ref, o_ref):
    merged = None
    for j, (o_br, w_br) in enumerate(((oa_ref, wa_ref), (ob_ref, wb_ref), (oc_ref, wc_ref))):
        y = jnp.dot(o_br[...], w_br[...], preferred_element_type=F32)
        t = gate_ref[:, j * D_MODEL:(j + 1) * D_MODEL].astype(F32) * y
        merged = t if merged is None else merged + t
    y = jnp.dot(merged.astype(BF16), wo_ref[...], preferred_element_type=F32)
    o_ref[...] = _post_norm_residual(x_ref[...], y, g_ref[...], mod_ref[0, 2:3, :])


def _merge(x, mod, g_post, oa, ob, oc, gates, wa, wb, wc, wo, l):
    tok = lambda i: (i, 0)
    return pl.pallas_call(
        _merge_kernel,
        grid=(TOKENS // TM,),
        in_specs=[pl.BlockSpec((TM, D_MODEL), tok),
                  pl.BlockSpec((1, 6, D_MODEL), lambda i: (i // TILES_PER_BATCH, 0, 0)),
                  _resident((1, D_MODEL)),
                  pl.BlockSpec((TM, oa.shape[1]), tok),
                  pl.BlockSpec((TM, ob.shape[1]), tok),
                  pl.BlockSpec((TM, oc.shape[1]), tok),
                  pl.BlockSpec((TM, GATE_WIDTH), tok),
                  _resident_layer(wa.shape, l), _resident_layer(wb.shape, l),
                  _resident_layer(wc.shape, l), _resident_layer(wo.shape, l)],
        out_specs=pl.BlockSpec((TM, D_MODEL), tok),
        out_shape=jax.ShapeDtypeStruct((TOKENS, D_MODEL), F32),
        compiler_params=_params(),
        name="merge_out",
    )(x, mod, g_post, oa, ob, oc, gates, wa, wb, wc, wo)


def _ffn_kernel(x_ref, mod_ref, gpre_ref, gpost_ref, wgu_ref, wd_ref, o_ref):
    for r0 in range(0, TM, SUB):
        x = x_ref[r0:r0 + SUB, :]
        hb = _modulated_norm(x, gpre_ref[...], mod_ref[0, 3:4, :], mod_ref[0, 4:5, :]).astype(BF16)
        a = jnp.dot(hb, wgu_ref[:, :D_FF], preferred_element_type=F32)
        b = jnp.dot(hb, wgu_ref[:, D_FF:], preferred_element_type=F32)
        act = (a * jax.nn.sigmoid(a) * b).astype(BF16)
        y = jnp.dot(act, wd_ref[...], preferred_element_type=F32)
        o_ref[r0:r0 + SUB, :] = _post_norm_residual(x, y, gpost_ref[...], mod_ref[0, 5:6, :])


def _ffn(x, mod, g_pre, g_post, wgu, wd, l):
    tok = lambda i: (i, 0)
    return pl.pallas_call(
        _ffn_kernel,
        grid=(TOKENS // TM,),
        in_specs=[pl.BlockSpec((TM, D_MODEL), tok),
                  pl.BlockSpec((1, 6, D_MODEL), lambda i: (i // TILES_PER_BATCH, 0, 0)),
                  _resident((1, D_MODEL)), _resident((1, D_MODEL)),
                  _resident_layer(wgu.shape, l), _resident_layer(wd.shape, l)],
        out_specs=pl.BlockSpec((TM, D_MODEL), tok),
        out_shape=jax.ShapeDtypeStruct((TOKENS, D_MODEL), F32),
        compiler_params=_params(),
        name="swiglu",
    )(x, mod, g_pre, g_post, wgu, wd)


def kernel(x, c, positions, w_ada, b_ada, g_pre_mix, g_post_mix, w_in, b_fgt, lam_q1, lam_k1,
           lam_q2, lam_k2, g_subln, w_br_a, w_br_b, w_br_c, w_out, g_pre_ffn, g_post_ffn,
           w_gate_up, w_down):
    cos, sin = _rope_tables(positions)
    mod = _ada_mod(c, w_ada, b_ada).reshape(DEPTH, BATCH, 6, D_MODEL)

    scale = HEAD_DIM ** -0.5 * LOG2E
    col = jnp.arange(W_ALL)
    is_q = (col < 512) | ((col >= 1536) & (col < 1792)) | ((col >= 2304) & (col < 2560))
    w_all = _win_prep(w_in, jnp.where(is_q, scale, 1.0).astype(F32)[None, :])
    wa, wb, wc, wo = (w.astype(BF16) for w in (w_br_a, w_br_b, w_br_c, w_out))
    wgu, wd = w_gate_up.astype(BF16), w_down.astype(BF16)

    xt = x.reshape(TOKENS, D_MODEL)
    for l in range(DEPTH):
        lam_init = 0.8 - 0.6 * math.exp(-0.3 * l)
        bf = jnp.pad(b_fgt[l], (0, LANES - N_HEADS))[None, :]
        qkv, gates, logf = _inproj(xt, mod[l], g_pre_mix[l][None, :], w_all, l, bf, cos, sin)

        qkv3 = qkv.reshape(BATCH, SEQ, QKV_WIDTH)
        lam_params = jnp.stack([lam_q1[l], lam_k1[l], lam_q2[l], lam_k2[l]])
        oa = _diff_attention(qkv3, lam_params, g_subln[l][None, :], lam_init)
        ob = _fox_attention(qkv3, logf.reshape(BATCH, SEQ, LANES))
        oc = _moba_attention(qkv3)

        xt = _merge(xt, mod[l], g_post_mix[l][None, :],
                    oa.reshape(TOKENS, -1), ob.reshape(TOKENS, -1), oc.reshape(TOKENS, -1), gates,
                    wa, wb, wc, wo, l)
        xt = _ffn(xt, mod[l], g_pre_ffn[l][None, :], g_post_ffn[l][None, :], wgu, wd, l)
    return xt.reshape(BATCH, SEQ, D_MODEL)
```

```python
import functools
import math

import jax
import jax.numpy as jnp
from jax import lax
from jax.experimental import pallas as pl
from jax.experimental.pallas import tpu as pltpu

D_MODEL = 1024
BATCH = 16
SEQ = 2048
DEPTH = 2
HEAD_DIM = 64
N_HEADS = 4
QKV_WIDTH = 3072
GATE_WIDTH = 3 * D_MODEL
D_FF = 2816
ROPE_THETA = 10000.0
MOBA_BLOCK = 256
MOBA_TOPK = 3
RMS_EPS = 1e-6
NEG_INF = -1e30
LOG2E = math.log2(math.e)

LANES = 128
TOKENS = BATCH * SEQ
TM = 1024
TILES_PER_BATCH = SEQ // TM
SUB = 256
TQ = 256
NQ = SEQ // TQ
N_KV_BLOCKS = SEQ // MOBA_BLOCK
VMEM_LIMIT = 56 * 1024 * 1024

COL_QA, COL_KA, COL_VA = 0, 4, 8
COL_QB, COL_KB, COL_VB = 12, 14, 16
COL_QC, COL_KC, COL_VC = 18, 20, 22
QKV_SECTIONS = (
    (0, 512, True), (512, 512, True), (1024, 512, False),
    (1536, 256, False), (1792, 256, False), (2048, 256, False),
    (2304, 256, True), (2560, 256, True), (2816, 256, False),
)

F32 = jnp.float32
BF16 = jnp.bfloat16


def _params(**kw):
    return pltpu.CompilerParams(vmem_limit_bytes=VMEM_LIMIT, **kw)


def _resident(shape):
    nd = len(shape)
    return pl.BlockSpec(shape, lambda *_: (0,) * nd, pipeline_mode=pl.Buffered(1))


def _resident_layer(shape, l):
    return pl.BlockSpec((None,) + tuple(shape[1:]), lambda *_: (l, 0, 0), pipeline_mode=pl.Buffered(1))


def _dot_nt(a, b):
    return lax.dot_general(a, b, (((1,), (1,)), ((), ())), preferred_element_type=F32)


def _split2(x):
    hi = x.astype(BF16)
    lo = (x - hi.astype(F32)).astype(BF16)
    return hi, lo


def _rope_table_kernel(pos_ref, invf_ref, sign_ref, cos_ref, sin_ref):
    ang = pos_ref[...].astype(F32) * invf_ref[...]
    cos_ref[...] = jnp.cos(ang)
    sin_ref[...] = jnp.sin(ang) * sign_ref[...]


def _rope_tables(positions):
    inv_freq = 1.0 / (ROPE_THETA ** (jnp.arange(0, HEAD_DIM, 2, dtype=F32) / HEAD_DIM))
    invf = jnp.tile(inv_freq, LANES // (HEAD_DIM // 2))[None, :]
    half = (jnp.arange(LANES) % HEAD_DIM) < HEAD_DIM // 2
    sign = jnp.where(half, -1.0, 1.0).astype(F32)[None, :]

    def tables(pos):
        n = pos.shape[0]
        return pl.pallas_call(
            _rope_table_kernel,
            grid=(n // SEQ,),
            in_specs=[pl.BlockSpec((SEQ, 1), lambda i: (i, 0)),
                      pl.BlockSpec((1, LANES), lambda i: (0, 0)),
                      pl.BlockSpec((1, LANES), lambda i: (0, 0))],
            out_specs=[pl.BlockSpec((SEQ, LANES), lambda i: (i, 0))] * 2,
            out_shape=[jax.ShapeDtypeStruct((n, LANES), F32)] * 2,
            compiler_params=_params(),
            name="rope_tables",
        )(pos.reshape(n, 1), invf, sign)

    def shared(pos):
        return tuple(jnp.tile(t, (BATCH, 1)) for t in tables(pos[0]))

    def per_row(pos):
        return tuple(tables(pos.reshape(TOKENS)))

    return lax.cond(jnp.all(positions == positions[:1]), shared, per_row, positions)


def _ada_kernel(c_ref, w_ref, b_ref, o_ref):
    c = c_ref[...]
    a = _split2(c * jax.nn.sigmoid(c))
    w = _split2(w_ref[0])
    acc = b_ref[0]
    for i, j in ((1, 0), (0, 1), (0, 0)):
        acc = acc + jnp.dot(a[i], w[j], preferred_element_type=F32)
    o_ref[0] = acc


def _ada_mod(c, w_ada, b_ada):
    tn = 1536
    n = 6 * D_MODEL
    return pl.pallas_call(
        _ada_kernel,
        grid=(DEPTH, n // tn),
        in_specs=[pl.BlockSpec((BATCH, D_MODEL), lambda l, j: (0, 0)),
                  pl.BlockSpec((1, D_MODEL, tn), lambda l, j: (l, 0, j)),
                  pl.BlockSpec((1, 1, tn), lambda l, j: (l, 0, j))],
        out_specs=pl.BlockSpec((1, BATCH, tn), lambda l, j: (l, 0, j)),
        out_shape=jax.ShapeDtypeStruct((DEPTH, BATCH, n), F32),
        compiler_params=_params(),
        name="ada_mod",
    )(c, w_ada, b_ada.reshape(DEPTH, 1, n))


D_IN = QKV_WIDTH + GATE_WIDTH + N_HEADS
W_ALL = QKV_WIDTH + GATE_WIDTH + LANES
W_BLOCK = 896


def _win_prep_kernel(w_ref, scale_ref, o_ref):
    col = pl.program_id(1) * W_BLOCK + lax.broadcasted_iota(jnp.int32, (1, W_BLOCK), 1)
    o_ref[0] = jnp.where(col < D_IN, w_ref[0] * scale_ref[...], 0.0).astype(BF16)


def _win_prep(w_in, col_scale):
    return pl.pallas_call(
        _win_prep_kernel,
        grid=(DEPTH, W_ALL // W_BLOCK),
        in_specs=[pl.BlockSpec((1, D_MODEL, W_BLOCK), lambda l, j: (l, 0, j)),
                  pl.BlockSpec((1, W_BLOCK), lambda l, j: (0, j))],
        out_specs=pl.BlockSpec((1, D_MODEL, W_BLOCK), lambda l, j: (l, 0, j)),
        out_shape=jax.ShapeDtypeStruct((DEPTH, D_MODEL, W_ALL), BF16),
        compiler_params=_params(),
        name="win_prep",
    )(w_in, col_scale)


def _modulated_norm(x, g, shift, scale):
    ms = jnp.mean(x * x, axis=-1, keepdims=True)
    return (x * lax.rsqrt(ms + RMS_EPS) * g) * (1.0 + scale) + shift


def _inproj_kernel(x_ref, mod_ref, g_ref, w_ref, bf_ref, cos_ref, sin_ref,
                   qkv_ref, gate_ref, logf_ref):
    lane = lax.broadcasted_iota(jnp.int32, (1, LANES), 1)
    first_half = (lane % HEAD_DIM) < HEAD_DIM // 2
    for r0 in range(0, TM, SUB):
        rows = slice(r0, r0 + SUB)
        h = _modulated_norm(x_ref[rows, :], g_ref[...], mod_ref[0, 0:1, :], mod_ref[0, 1:2, :])
        hb = h.astype(BF16)
        cos = cos_ref[rows, :]
        sin = sin_ref[rows, :]

        def rope(t):
            rot = jnp.where(first_half, pltpu.roll(t, LANES - HEAD_DIM // 2, 1),
                            pltpu.roll(t, HEAD_DIM // 2, 1))
            return t * cos + rot * sin

        for c0, width, roped in QKV_SECTIONS:
            acc = jnp.dot(hb, w_ref[:, c0:c0 + width], preferred_element_type=F32)
            if roped:
                for g in range(width // LANES):
                    qkv_ref[rows, c0 + g * LANES:c0 + (g + 1) * LANES] = rope(
                        acc[:, g * LANES:(g + 1) * LANES]).astype(BF16)
            else:
                qkv_ref[rows, c0:c0 + width] = acc.astype(BF16)

        for c0 in range(0, GATE_WIDTH, D_MODEL):
            acc = jnp.dot(hb, w_ref[:, QKV_WIDTH + c0:QKV_WIDTH + c0 + D_MODEL],
                          preferred_element_type=F32)
            gate_ref[rows, c0:c0 + D_MODEL] = jax.nn.sigmoid(acc).astype(BF16)

        z = jnp.dot(hb, w_ref[:, QKV_WIDTH + GATE_WIDTH:], preferred_element_type=F32) + bf_ref[...]
        logf_ref[rows, :] = jnp.minimum(z, 0.0) - jnp.log1p(jnp.exp(-jnp.abs(z)))


def _inproj(x, mod, g_pre, w_all, l, b_fgt, cos, sin):
    tok = lambda i: (i, 0)
    return pl.pallas_call(
        _inproj_kernel,
        grid=(TOKENS // TM,),
        in_specs=[pl.BlockSpec((TM, D_MODEL), tok),
                  pl.BlockSpec((1, 6, D_MODEL), lambda i: (i // TILES_PER_BATCH, 0, 0)),
                  _resident((1, D_MODEL)),
                  _resident_layer(w_all.shape, l),
                  _resident((1, LANES)),
                  pl.BlockSpec((TM, LANES), tok),
                  pl.BlockSpec((TM, LANES), tok)],
        out_specs=[pl.BlockSpec((TM, QKV_WIDTH), tok),
                   pl.BlockSpec((TM, GATE_WIDTH), tok),
                   pl.BlockSpec((TM, LANES), tok)],
        out_shape=[jax.ShapeDtypeStruct((TOKENS, QKV_WIDTH), BF16),
                   jax.ShapeDtypeStruct((TOKENS, GATE_WIDTH), BF16),
                   jax.ShapeDtypeStruct((TOKENS, LANES), F32)],
        compiler_params=_params(),
        name="inproj",
    )(x, mod, g_pre, w_all, b_fgt, cos, sin)


FAST_ROWS = 256
F32_HUGE = 3.0e38


def _causal(s, row_offset):
    row = lax.broadcasted_iota(jnp.int32, s.shape, 0) + row_offset
    col = lax.broadcasted_iota(jnp.int32, s.shape, 1)
    return jnp.where(col <= row, s, NEG_INF)


def _fast_unit(q, diag_bias, bias, k_ref, vaug_ref, i, row0):
    r = q.shape[0]
    m = jnp.sum(q.astype(F32) * k_ref[row0:row0 + r, :].astype(F32), axis=-1, keepdims=True)
    if diag_bias is not None:
        m = m + diag_bias
    acc = None
    for c in range(i + 1):
        s = _dot_nt(q, k_ref[c * TQ:(c + 1) * TQ, :])
        shift = m
        b = bias(c)
        if b is not None and b.shape[1] == 1:
            shift = m - b
        elif b is not None:
            s = s + b
        if c == i:
            s = _causal(s, row0 - i * TQ)
        p = jnp.exp2(s - shift).astype(BF16)
        t = jnp.dot(p, vaug_ref[c * TQ:(c + 1) * TQ, :], preferred_element_type=F32)
        acc = t if acc is None else acc + t
    return acc


def _normalised(o_aug):
    return o_aug[:, :LANES] / o_aug[:, LANES:]


def _attend_pairs(prepare, unit_terms, k_refs, vaug_ref, s_ref, p_ref, finish):
    overflowed = jnp.zeros((FAST_ROWS, 2 * LANES), F32)
    prepare(0)
    for i in range(NQ):
        if i + 1 < NQ:
            prepare(i + 1)
        for r0 in range(i * TQ, (i + 1) * TQ, FAST_ROWS):
            accs = [_fast_unit(*unit_terms(half, r0, FAST_ROWS), k_refs[half], vaug_ref, i, r0)
                    for half in (0, 1)]
            finite = jnp.abs(accs[0] + accs[1]) < F32_HUGE
            overflowed = overflowed + jnp.where(finite, 0.0, 1.0)
            finish(r0, _normalised(accs[0]), _normalised(accs[1]))

    @pl.when(jnp.sum(overflowed) > 0.0)
    def _():
        _attend_pairs_two_pass(unit_terms, k_refs, vaug_ref, s_ref, p_ref, finish)


def _scores(q, bias, k_ref, s_ref, i):
    for c in range(i + 1):
        s = _dot_nt(q, k_ref[c * TQ:(c + 1) * TQ, :])
        b = bias(c)
        if b is not None:
            s = s + b
        if c == i:
            s = _causal(s, 0)
        s_ref[:, c * TQ:(c + 1) * TQ] = s


def _softmax_pv(s_ref, p_ref, vaug_ref, i):
    w = (i + 1) * TQ
    n_groups = w // LANES
    mm = s_ref[:, 0:LANES]
    for g in range(1, n_groups):
        mm = jnp.maximum(mm, s_ref[:, g * LANES:(g + 1) * LANES])
    m = jnp.max(mm, axis=-1, keepdims=True)
    for g in range(n_groups):
        p = jnp.exp2(s_ref[:, g * LANES:(g + 1) * LANES] - m)
        p_ref[:, g * LANES:(g + 1) * LANES] = p.astype(BF16)
    return jnp.dot(p_ref[:, :w], vaug_ref[:w, :], preferred_element_type=F32)


def _attend_pairs_two_pass(unit_terms, k_refs, vaug_ref, s_ref, p_ref, finish):
    units = [(i, half) for i in range(NQ) for half in (0, 1)]

    def scores(n):
        i, half = units[n]
        q, _, bias = unit_terms(half, i * TQ, TQ)
        _scores(q, bias, k_refs[half], s_ref.at[half], i)

    scores(0)
    outs = [None, None]
    for n, (i, half) in enumerate(units):
        if n + 1 < len(units):
            scores(n + 1)
        outs[half] = _normalised(_softmax_pv(s_ref.at[half], p_ref.at[half], vaug_ref, i))
        if half == 1:
            finish(i * TQ, outs[0], outs[1])


def _fill_vaug(vaug_ref, v_ref, i):
    rows = slice(i * TQ, (i + 1) * TQ)
    vaug_ref[rows, :LANES] = v_ref[rows, :]
    vaug_ref[rows, LANES:] = jnp.ones((TQ, LANES), BF16)


def _head_masks(shape):
    lane = lax.broadcasted_iota(jnp.int32, shape, 1)
    return lane < HEAD_DIM, lane >= HEAD_DIM


def _masked_q(q_ref, half, row0, rows):
    q2 = q_ref[row0:row0 + rows, :]
    return jnp.where(_head_masks(q2.shape)[half], q2, jnp.zeros_like(q2))


def _attn_call(kernel, extra_inputs, extra_specs, qkv, cols, n_pairs, scratch, name):
    cq, ck, cv = cols
    blk = (None, SEQ, LANES)
    return pl.pallas_call(
        kernel,
        grid=(BATCH, n_pairs),
        in_specs=list(extra_specs) + [
            pl.BlockSpec(blk, lambda b, p: (b, 0, cq + p)),
            pl.BlockSpec(blk, lambda b, p: (b, 0, ck + p)),
            pl.BlockSpec(blk, lambda b, p: (b, 0, cv + p))],
        out_specs=pl.BlockSpec(blk, lambda b, p: (b, 0, p)),
        out_shape=jax.ShapeDtypeStruct((BATCH, SEQ, n_pairs * LANES), BF16),
        scratch_shapes=[pltpu.VMEM((2, TQ, SEQ), F32), pltpu.VMEM((2, TQ, SEQ), BF16),
                        pltpu.VMEM((SEQ, 2 * LANES), BF16)] + list(scratch),
        compiler_params=_params(),
        name=name,
    )(*extra_inputs, qkv, qkv, qkv)


def _write_heads(o_ref, row0, o1, o2):
    o_ref[row0:row0 + o1.shape[0], :] = jnp.where(_head_masks(o1.shape)[0], o1, o2).astype(BF16)


def _diff_kernel(lam_ref, gsub_ref, q_ref, k_ref, v_ref, o_ref, s_ref, p_ref, vaug_ref, *, lam_init):
    lp = lam_ref[...]
    lam = (jnp.exp(jnp.sum(lp[0:1] * lp[1:2], axis=-1, keepdims=True))
           - jnp.exp(jnp.sum(lp[2:3] * lp[3:4], axis=-1, keepdims=True)) + lam_init)

    def prepare(i):
        _fill_vaug(vaug_ref, v_ref, i)

    def unit_terms(half, row0, rows):
        return _masked_q(q_ref, half, row0, rows), None, lambda c: None

    def finish(row0, o1, o2):
        o = o1 - lam * o2
        ms = jnp.mean(o * o, axis=-1, keepdims=True)
        o = (o * lax.rsqrt(ms + RMS_EPS) * gsub_ref[...]) * (1.0 - lam_init)
        o_ref[row0:row0 + o.shape[0], :] = o.astype(BF16)

    _attend_pairs(prepare, unit_terms, (k_ref, k_ref), vaug_ref, s_ref, p_ref, finish)


def _diff_attention(qkv, lam_params, g_subln, lam_init):
    return _attn_call(
        functools.partial(_diff_kernel, lam_init=lam_init),
        (lam_params, g_subln),
        (pl.BlockSpec((4, HEAD_DIM), lambda b, p: (0, 0)),
         pl.BlockSpec((1, 2 * HEAD_DIM), lambda b, p: (0, 0))),
        qkv, (COL_QA, COL_KA, COL_VA), N_HEADS, (), "diff_attention")


def _fox_kernel(logf_ref, q_ref, k_ref, v_ref, o_ref, s_ref, p_ref, vaug_ref, cum_ref, nckt_ref):
    pair = pl.program_id(1)
    tri = (lax.broadcasted_iota(jnp.int32, (TQ, TQ), 0)
           >= lax.broadcasted_iota(jnp.int32, (TQ, TQ), 1)).astype(BF16)
    lane = lax.broadcasted_iota(jnp.int32, (1, LANES), 1)
    carry = [jnp.zeros((1, LANES), F32)]

    def prepare(i):
        rows = slice(i * TQ, (i + 1) * TQ)
        _fill_vaug(vaug_ref, v_ref, i)
        lf = logf_ref[rows, :]
        hi, lo = _split2(lf)
        r = jnp.dot(tri, jnp.concatenate([hi, lo], axis=1), preferred_element_type=F32)
        cum = (r[:, :LANES] + r[:, LANES:]) + carry[0]
        carry[0] = cum[TQ - 1:TQ, :]
        cum2 = cum * LOG2E
        cum_ref[rows, :] = cum2
        nckt_ref[:, rows] = jnp.transpose(-cum2)[:N_HEADS * 2, :]

    def unit_terms(half, row0, rows):
        head = 2 * pair + half
        own = jnp.where(lane == head, cum_ref[row0:row0 + rows, :], 0.0)
        diag_bias = -jnp.sum(own, axis=-1, keepdims=True)
        return (_masked_q(q_ref, half, row0, rows), diag_bias,
                lambda c: nckt_ref[pl.ds(head, 1), c * TQ:(c + 1) * TQ])

    _attend_pairs(prepare, unit_terms, (k_ref, k_ref), vaug_ref, s_ref, p_ref,
                  functools.partial(_write_heads, o_ref))


def _fox_attention(qkv, logf):
    return _attn_call(
        _fox_kernel, (logf,),
        (pl.BlockSpec((None, SEQ, LANES), lambda b, p: (b, 0, 0)),),
        qkv, (COL_QB, COL_KB, COL_VB), N_HEADS // 2,
        (pltpu.VMEM((SEQ, LANES), F32), pltpu.VMEM((2 * N_HEADS, SEQ), F32)),
        "fox_attention")


def _moba_kernel(q_ref, k_ref, v_ref, o_ref, s_ref, p_ref, vaug_ref, qhard_ref, kaug_ref):
    for i in range(NQ):
        _fill_vaug(vaug_ref, v_ref, i)
    member = (lax.broadcasted_iota(jnp.int32, (N_KV_BLOCKS, SEQ), 1) // MOBA_BLOCK
              == lax.broadcasted_iota(jnp.int32, (N_KV_BLOCKS, SEQ), 0)).astype(BF16)
    kmean = jnp.dot(member, k_ref[...], preferred_element_type=F32) * (1.0 / MOBA_BLOCK)
    km_hi = kmean.astype(BF16)
    km_lo = (kmean - km_hi.astype(F32)).astype(BF16)
    kmx = jnp.concatenate([km_hi, km_lo], axis=1)

    n_easy = (MOBA_TOPK + 1) * MOBA_BLOCK
    n_hard = SEQ - n_easy
    lane_row = lax.broadcasted_iota(jnp.int32, (1, LANES), 1)
    blk = lax.broadcasted_iota(jnp.int32, (N_KV_BLOCKS, n_hard), 0)
    blk_f = blk.astype(F32)
    past = blk < lax.broadcasted_iota(jnp.int32, (N_KV_BLOCKS, n_hard), 1) // MOBA_BLOCK + (MOBA_TOPK + 1)
    for half in (0, 1):
        base = HEAD_DIM * (1 - half)
        own = _head_masks((TQ, LANES))[half]
        for n in range(N_KV_BLOCKS):
            rows = slice(n * MOBA_BLOCK, (n + 1) * MOBA_BLOCK)
            one_hot = jnp.where(lane_row == base + n, 1.0, 0.0).astype(BF16)
            kaug_ref[half, rows, :] = jnp.where(own, k_ref[rows, :], one_hot)
        qh = _masked_q(q_ref, half, n_easy, n_hard)
        gate = _dot_nt(kmx, jnp.concatenate([qh, qh], axis=1))
        gate = jnp.where(past, gate, NEG_INF)
        drop = jnp.where(past, -1.0, 0.0)
        for _ in range(MOBA_TOPK):
            top = jnp.max(gate, axis=0, keepdims=True)
            idx = jnp.min(jnp.where(gate == top, blk_f, float(N_KV_BLOCKS)), axis=0, keepdims=True)
            hit = blk_f == idx
            drop = jnp.where(hit, 0.0, drop)
            gate = jnp.where(hit, NEG_INF, gate)
        place = (lax.broadcasted_iota(jnp.int32, (N_KV_BLOCKS, LANES), 1)
                 == lax.broadcasted_iota(jnp.int32, (N_KV_BLOCKS, LANES), 0) + base).astype(BF16)
        dropped = lax.dot_general(drop.astype(BF16), place, (((0,), (0,)), ((), ())),
                                  preferred_element_type=F32)
        own_h = _head_masks((n_hard, LANES))[half]
        qhard_ref[half] = jnp.where(own_h, q_ref[n_easy:, :], (dropped * -NEG_INF).astype(BF16))

    def unit_terms(half, row0, rows):
        if row0 < n_easy:
            return _masked_q(q_ref, half, row0, rows), None, lambda c: None
        return qhard_ref[half, row0 - n_easy:row0 - n_easy + rows, :], None, lambda c: None

    _attend_pairs(lambda i: None, unit_terms, (kaug_ref.at[0], kaug_ref.at[1]), vaug_ref, s_ref, p_ref,
                  functools.partial(_write_heads, o_ref))


def _moba_attention(qkv):
    return _attn_call(
        _moba_kernel, (), (), qkv, (COL_QC, COL_KC, COL_VC), N_HEADS // 2,
        (pltpu.VMEM((2, SEQ - (MOBA_TOPK + 1) * MOBA_BLOCK, LANES), BF16),
         pltpu.VMEM((2, SEQ, LANES), BF16)),
        "moba_attention")


def _post_norm_residual(x, y, g, gate):
    ms = jnp.mean(y * y, axis=-1, keepdims=True)
    return x + gate * (y * lax.rsqrt(ms + RMS_EPS) * g)


def _merge_kernel(x_ref, mod_ref, g_ref, oa_ref, ob_ref, oc_ref, gate_ref,
                  wa_ref, wb_ref, wc_ref, wo_ref, o_ref):
    merged = None
    for j, (o_br, w_br) in enumerate(((oa_ref, wa_ref), (ob_ref, wb_ref), (oc_ref, wc_ref))):
        y = jnp.dot(o_br[...], w_br[...], preferred_element_type=F32)
        t = gate_ref[:, j * D_MODEL:(j + 1) * D_MODEL].astype(F32) * y
        merged = t if merged is None else merged + t
    y = jnp.dot(merged.astype(BF16), wo_ref[...], preferred_element_type=F32)
    o_ref[...] = _post_norm_residual(x_ref[...], y, g_ref[...], mod_ref[0, 2:3, :])


def _merge(x, mod, g_post, oa, ob, oc, gates, wa, wb, wc, wo, l):
    tok = lambda i: (i, 0)
    return pl.pallas_call(
        _merge_kernel,
        grid=(TOKENS // TM,),
        in_specs=[pl.BlockSpec((TM, D_MODEL), tok),
                  pl.BlockSpec((1, 6, D_MODEL), lambda i: (i // TILES_PER_BATCH, 0, 0)),
                  _resident((1, D_MODEL)),
                  pl.BlockSpec((TM, oa.shape[1]), tok),
                  pl.BlockSpec((TM, ob.shape[1]), tok),
                  pl.BlockSpec((TM, oc.shape[1]), tok),
                  pl.BlockSpec((TM, GATE_WIDTH), tok),
                  _resident_layer(wa.shape, l), _resident_layer(wb.shape, l),
                  _resident_layer(wc.shape, l), _resident_layer(wo.shape, l)],
        out_specs=pl.BlockSpec((TM, D_MODEL), tok),
        out_shape=jax.ShapeDtypeStruct((TOKENS, D_MODEL), F32),
        compiler_params=_params(),
        name="merge_out",
    )(x, mod, g_post, oa, ob, oc, gates, wa, wb, wc, wo)


def _ffn_kernel(x_ref, mod_ref, gpre_ref, gpost_ref, wgu_ref, wd_ref, o_ref):
    for r0 in range(0, TM, SUB):
        x = x_ref[r0:r0 + SUB, :]
        hb = _modulated_norm(x, gpre_ref[...], mod_ref[0, 3:4, :], mod_ref[0, 4:5, :]).astype(BF16)
        a = jnp.dot(hb, wgu_ref[:, :D_FF], preferred_element_type=F32)
        b = jnp.dot(hb, wgu_ref[:, D_FF:], preferred_element_type=F32)
        act = (a * jax.nn.sigmoid(a) * b).astype(BF16)
        y = jnp.dot(act, wd_ref[...], preferred_element_type=F32)
        o_ref[r0:r0 + SUB, :] = _post_norm_residual(x, y, gpost_ref[...], mod_ref[0, 5:6, :])


def _ffn(x, mod, g_pre, g_post, wgu, wd, l):
    tok = lambda i: (i, 0)
    return pl.pallas_call(
        _ffn_kernel,
        grid=(TOKENS // TM,),
        in_specs=[pl.BlockSpec((TM, D_MODEL), tok),
                  pl.BlockSpec((1, 6, D_MODEL), lambda i: (i // TILES_PER_BATCH, 0, 0)),
                  _resident((1, D_MODEL)), _resident((1, D_MODEL)),
                  _resident_layer(wgu.shape, l), _resident_layer(wd.shape, l)],
        out_specs=pl.BlockSpec((TM, D_MODEL), tok),
        out_shape=jax.ShapeDtypeStruct((TOKENS, D_MODEL), F32),
        compiler_params=_params(),
        name="swiglu",
    )(x, mod, g_pre, g_post, wgu, wd)


def kernel(x, c, positions, w_ada, b_ada, g_pre_mix, g_post_mix, w_in, b_fgt, lam_q1, lam_k1,
           lam_q2, lam_k2, g_subln, w_br_a, w_br_b, w_br_c, w_out, g_pre_ffn, g_post_ffn,
           w_gate_up, w_down):
    cos, sin = _rope_tables(positions)
    mod = _ada_mod(c, w_ada, b_ada).reshape(DEPTH, BATCH, 6, D_MODEL)

    scale = HEAD_DIM ** -0.5 * LOG2E
    col = jnp.arange(W_ALL)
    is_q = functools.reduce(jnp.logical_or, [(col >= c0) & (col < c0 + width)
                                             for c0, width, _ in QKV_SECTIONS[0::3]])
    w_all = _win_prep(w_in, jnp.where(is_q, scale, 1.0).astype(F32)[None, :])
    wa, wb, wc, wo = (w.astype(BF16) for w in (w_br_a, w_br_b, w_br_c, w_out))
    wgu, wd = w_gate_up.astype(BF16), w_down.astype(BF16)

    xt = x.reshape(TOKENS, D_MODEL)
    for l in range(DEPTH):
        lam_init = 0.8 - 0.6 * math.exp(-0.3 * l)
        bf = jnp.pad(b_fgt[l], (0, LANES - N_HEADS))[None, :]
        qkv, gates, logf = _inproj(xt, mod[l], g_pre_mix[l][None, :], w_all, l, bf, cos, sin)

        qkv3 = qkv.reshape(BATCH, SEQ, QKV_WIDTH)
        lam_params = jnp.stack([lam_q1[l], lam_k1[l], lam_q2[l], lam_k2[l]])
        oa = _diff_attention(qkv3, lam_params, g_subln[l][None, :], lam_init)
        ob = _fox_attention(qkv3, logf.reshape(BATCH, SEQ, LANES))
        oc = _moba_attention(qkv3)

        xt = _merge(xt, mod[l], g_post_mix[l][None, :],
                    oa.reshape(TOKENS, -1), ob.reshape(TOKENS, -1), oc.reshape(TOKENS, -1), gates,
                    wa, wb, wc, wo, l)
        xt = _ffn(xt, mod[l], g_pre_ffn[l][None, :], g_post_ffn[l][None, :], wgu, wd, l)
    return xt.reshape(BATCH, SEQ, D_MODEL)
```
